```python
import jax, jax.numpy as jnp
from jax import lax
import numpy as np

D_MODEL = 2048
BATCH = 1
SEQ = 16384
DEPTH = 2

GRID_W = 64
CTX_LEN = 256
EPS = 1e-6
NA_HEADS = 8
NA_HEAD_DIM = 64
NA_WIN_H_MAX = 8
NA_WIN_W = 16
NA_W = NA_HEADS * NA_HEAD_DIM
MLA_HEADS = 4
MLA_NOPE = 128
MLA_ROPE = 64
MLA_V = 128
MLA_KV_RANK = 256
MLA_QK = MLA_NOPE + MLA_ROPE
ROPE_NFREQ = MLA_ROPE // 4
ROPE_BASE = 10000.0
Q_BLOCK = 128
CONV_CH = 512
CONV_K = 3
D_FF = 5632
OFF_NA_K = 0
OFF_NA_V = OFF_NA_K + NA_W
OFF_CKV = OFF_NA_V + NA_W
OFF_KROPE = OFF_CKV + MLA_KV_RANK
KV_COLS = OFF_KROPE + MLA_ROPE
OFF_NA_Q = KV_COLS
OFF_MLA_Q = OFF_NA_Q + NA_W
OFF_CONV_U = OFF_MLA_Q + MLA_HEADS * MLA_QK
OFF_CONV_B = OFF_CONV_U + CONV_CH
OFF_CONV_C = OFF_CONV_B + CONV_CH
OFF_GATE = OFF_CONV_C + CONV_CH
IN_COLS = OFF_GATE + 3 * D_MODEL

kernel_name = 'hybrid_na_mla_shortconv_prefix_dit'


def rmsnorm(x, g):
    xf = x.astype(jnp.float32)
    y = xf * lax.rsqrt(jnp.mean(xf * xf, axis=-1, keepdims=True) + EPS)
    return (y * g.astype(jnp.float32)).astype(x.dtype)


def dwconv(x, w):
    ch = x.shape[-1]
    return lax.conv_general_dilated(
        x, w[:, None, :].astype(x.dtype), window_strides=(1,),
        padding=[(CONV_K // 2, CONV_K // 2)],
        dimension_numbers=('NWC', 'WIO', 'NWC'), feature_group_count=ch)


def rope_tables(n_tok):
    t = jnp.arange(n_tok)
    row = (t // GRID_W).astype(jnp.float32)
    col = (t % GRID_W).astype(jnp.float32)
    inv = ROPE_BASE ** (-jnp.arange(ROPE_NFREQ, dtype=jnp.float32) / ROPE_NFREQ)
    ang = jnp.stack([row[:, None] * inv, col[:, None] * inv], axis=1)
    return jnp.cos(ang), jnp.sin(ang)


def rope_2d(x, cos, sin):
    shp = x.shape
    xf = x.astype(jnp.float32).reshape(shp[:-1] + (2, 2, ROPE_NFREQ))
    x1, x2 = xf[..., 0, :], xf[..., 1, :]
    cs = cos[None, :, None]
    sn = sin[None, :, None]
    out = jnp.stack([x1 * cs - x2 * sn, x2 * cs + x1 * sn], axis=-2).reshape(shp)
    return out.astype(x.dtype)


def with_rope(x, cos, sin):
    return jnp.concatenate([x[..., :MLA_NOPE], rope_2d(x[..., MLA_NOPE:], cos, sin)], axis=-1)


def kv_side(proj, na_k_g, mla_kv_g, w_kv_up, mla_k_g):
    b, n = proj.shape[0], proj.shape[1]
    na_k = rmsnorm(proj[..., OFF_NA_K:OFF_NA_V].reshape(b, n, NA_HEADS, NA_HEAD_DIM), na_k_g)
    na_v = proj[..., OFF_NA_V:OFF_CKV].reshape(b, n, NA_HEADS, NA_HEAD_DIM)
    ckv = rmsnorm(proj[..., OFF_CKV:OFF_KROPE], mla_kv_g)
    kv = (ckv @ w_kv_up).reshape(b, n, MLA_HEADS, MLA_NOPE + MLA_V)
    k_rope = jnp.broadcast_to(proj[..., OFF_KROPE:KV_COLS][:, :, None, :], (b, n, MLA_HEADS, MLA_ROPE))
    mla_k = rmsnorm(jnp.concatenate([kv[..., :MLA_NOPE], k_rope], axis=-1), mla_k_g)
    return na_k, na_v, mla_k, kv[..., MLA_NOPE:]


def q_side(proj, na_q_g, mla_q_g):
    b, n = proj.shape[0], proj.shape[1]
    na_q = rmsnorm(proj[..., OFF_NA_Q:OFF_MLA_Q].reshape(b, n, NA_HEADS, NA_HEAD_DIM), na_q_g)
    mla_q = rmsnorm(proj[..., OFF_MLA_Q:OFF_CONV_U].reshape(b, n, MLA_HEADS, MLA_QK), mla_q_g)
    return na_q, mla_q


def full_attention(q, k, v):
    b, n, h, _ = q.shape
    q = q * (q.shape[-1] ** -0.5)
    s = jnp.einsum('bqhd,bkhd->bhqk', q, k).astype(jnp.float32)
    p = jax.nn.softmax(s, axis=-1).astype(v.dtype)
    return jnp.einsum('bhqk,bkhd->bqhd', p, v).reshape(b, n, h * v.shape[-1])


def dense_latent_attention(q, k_all, v_all):
    b, n, h, dq = q.shape
    dv = v_all.shape[-1]
    qb = jnp.moveaxis((q * (dq ** -0.5)).reshape(b, n // Q_BLOCK, Q_BLOCK, h, dq), 1, 0)

    def block(qi):
        s = jnp.einsum('bqhd,bkhd->bhqk', qi, k_all).astype(jnp.float32)
        p = jax.nn.softmax(s, axis=-1).astype(v_all.dtype)
        return jnp.einsum('bhqk,bkhd->bqhd', p, v_all)

    out = lax.map(block, qb)
    return jnp.moveaxis(out, 0, 1).reshape(b, n, h * dv)


def na_latent(q, k, v, kc, vc, rel_bias):
    b, n, h, d = q.shape
    rows = n // GRID_W
    win_h = min(NA_WIN_H_MAX, rows)
    qg = (q * (d ** -0.5)).reshape(b, rows, GRID_W, h, d)
    kg = k.reshape(b, rows, GRID_W, h, d)
    vg = v.reshape(b, rows, GRID_W, h, d)
    col = jnp.arange(GRID_W)
    start_c = jnp.clip(col - NA_WIN_W // 2, 0, GRID_W - NA_WIN_W)
    col_idx = start_c[:, None] + jnp.arange(NA_WIN_W)[None, :]
    dc_idx = col_idx - col[:, None] + (NA_WIN_W - 1)
    n_loc = win_h * NA_WIN_W

    def row_block(args):
        r, q_row = args
        start = jnp.clip(r - win_h // 2, 0, rows - win_h)
        k_win = lax.dynamic_slice_in_dim(kg, start, win_h, axis=1)[:, :, col_idx]
        v_win = lax.dynamic_slice_in_dim(vg, start, win_h, axis=1)[:, :, col_idx]
        dr_idx = start + jnp.arange(win_h) - r + (NA_WIN_H_MAX - 1)
        bias = rel_bias[:, dr_idx[None, :, None], dc_idx[:, None, :]].astype(jnp.float32)
        s_loc = jnp.einsum('bqhd,biqjhd->bhqij', q_row, k_win).astype(jnp.float32) + bias[None]
        s_ctx = jnp.einsum('bqhd,bchd->bhqc', q_row, kc).astype(jnp.float32)
        s = jnp.concatenate([s_loc.reshape(b, h, GRID_W, n_loc), s_ctx], axis=-1)
        p = jax.nn.softmax(s, axis=-1).astype(v.dtype)
        p_loc = p[..., :n_loc].reshape(b, h, GRID_W, win_h, NA_WIN_W)
        return (jnp.einsum('bhqij,biqjhd->bqhd', p_loc, v_win)
                + jnp.einsum('bhqc,bchd->bqhd', p[..., n_loc:], vc))

    out = lax.map(row_block, (jnp.arange(rows), jnp.moveaxis(qg, 1, 0)))
    return jnp.moveaxis(out, 0, 1).reshape(b, n, h * d)


def short_conv(proj, w):
    u = proj[..., OFF_CONV_U:OFF_CONV_B]
    gb = proj[..., OFF_CONV_B:OFF_CONV_C]
    gc = proj[..., OFF_CONV_C:OFF_GATE]
    return gb * dwconv(gc * u, w)


def merge_branches(ya, yb, yc, gate_cols, w_br_a, w_br_b, w_br_c, w_o):
    ga, gb, gc = jnp.split(jax.nn.sigmoid(gate_cols), 3, axis=-1)
    return (ga * (ya @ w_br_a) + gb * (yb @ w_br_b) + gc * (yc @ w_br_c)) @ w_o


def conv_glu_ffn(h, w_up, conv_w, conv_b, w_down):
    a, val = jnp.split(h @ w_up, 2, axis=-1)
    a = dwconv(a, conv_w) + conv_b
    return (jax.nn.silu(a) * val) @ w_down


def setup_inputs(seed: int = 0) -> dict:
    key = jax.random.key(seed)
    ks = jax.random.split(key, 32)
    L = DEPTH
    D = D_MODEL

    def nrm(k, shape, scale):
        return jax.random.normal(k, shape, jnp.float32) * scale

    return {
        'x': nrm(ks[0], (BATCH, SEQ, D), 1.0),
        'c': nrm(ks[1], (BATCH, D), 1.0),
        'ctx': nrm(ks[2], (BATCH, CTX_LEN, D), 1.0),
        'c_ctx': nrm(ks[3], (D,), 1.0),
        'w_mod': nrm(ks[4], (L, D, 6 * D), 0.5 * D ** -0.5),
        'b_mod': nrm(ks[5], (L, 6 * D), 0.01),
        'norm1_g': 1.0 + nrm(ks[6], (L, D), 0.02),
        'w_in': nrm(ks[7], (L, D, IN_COLS), D ** -0.5),
        'na_q_g': 1.0 + nrm(ks[8], (L, NA_HEAD_DIM), 0.02),
        'na_k_g': 1.0 + nrm(ks[9], (L, NA_HEAD_DIM), 0.02),
        'na_rel_bias': nrm(ks[10], (L, NA_HEADS, 2 * NA_WIN_H_MAX - 1, 2 * NA_WIN_W - 1), 0.1),
        'mla_kv_g': 1.0 + nrm(ks[11], (L, MLA_KV_RANK), 0.02),
        'w_kv_up': nrm(ks[12], (L, MLA_KV_RANK, MLA_HEADS * (MLA_NOPE + MLA_V)), MLA_KV_RANK ** -0.5),
        'mla_q_g': 1.0 + nrm(ks[13], (L, MLA_QK), 0.02),
        'mla_k_g': 1.0 + nrm(ks[14], (L, MLA_QK), 0.02),
        'conv_w': nrm(ks[15], (L, CONV_K, CONV_CH), CONV_K ** -0.5),
        'w_br_a': nrm(ks[16], (L, NA_W, D), NA_W ** -0.5),
        'w_br_b': nrm(ks[17], (L, MLA_HEADS * MLA_V, D), (MLA_HEADS * MLA_V) ** -0.5),
        'w_br_c': nrm(ks[18], (L, CONV_CH, D), CONV_CH ** -0.5),
        'w_o': nrm(ks[19], (L, D, D), D ** -0.5),
        'norm2_g': 1.0 + nrm(ks[20], (L, D), 0.02),
        'w_up': nrm(ks[21], (L, D, 2 * D_FF), D ** -0.5),
        'ffn_conv_w': nrm(ks[22], (L, CONV_K, D_FF), CONV_K ** -0.5),
        'ffn_conv_b': nrm(ks[23], (L, D_FF), 0.01),
        'w_down': nrm(ks[24], (L, D_FF, D), D_FF ** -0.5),
    }


def reference(x, c, ctx, c_ctx, w_mod, b_mod, norm1_g, w_in, na_q_g, na_k_g, na_rel_bias,
              mla_kv_g, w_kv_up, mla_q_g, mla_k_g, conv_w, w_br_a, w_br_b, w_br_c, w_o,
              norm2_g, w_up, ffn_conv_w, ffn_conv_b, w_down):
    cos, sin = rope_tables(x.shape[1])
    for l in range(DEPTH):
        last = l == DEPTH - 1
        mx = (jax.nn.silu(c) @ w_mod[l] + b_mod[l])[:, None, :]
        mc = jax.nn.silu(c_ctx) @ w_mod[l] + b_mod[l]
        sh1, sc1, g1, sh2, sc2, g2 = jnp.split(mx, 6, axis=-1)
        csh1, csc1, cg1, csh2, csc2, cg2 = jnp.split(mc, 6, axis=-1)

        hx = rmsnorm(x, norm1_g[l]) * (1.0 + sc1) + sh1
        hc = rmsnorm(ctx, norm1_g[l]) * (1.0 + csc1) + csh1
        px = hx @ w_in[l]
        pc = hc @ (w_in[l][:, :KV_COLS] if last else w_in[l])
        kv_args = (na_k_g[l], mla_kv_g[l], w_kv_up[l], mla_k_g[l])
        na_kc, na_vc, mla_kc, mla_vc = kv_side(pc, *kv_args)
        na_k, na_v, mla_k, mla_v = kv_side(px, *kv_args)
        na_q, mla_q = q_side(px, na_q_g[l], mla_q_g[l])
        mla_q = with_rope(mla_q, cos, sin)
        mla_k = with_rope(mla_k, cos, sin)

        ya = na_latent(na_q, na_k, na_v, na_kc, na_vc, na_rel_bias[l])
        yb = dense_latent_attention(mla_q, jnp.concatenate([mla_k, mla_kc], axis=1),
                                    jnp.concatenate([mla_v, mla_vc], axis=1))
        yc = short_conv(px, conv_w[l])
        x_mix = merge_branches(ya, yb, yc, px[..., OFF_GATE:], w_br_a[l], w_br_b[l], w_br_c[l], w_o[l])
        x = x + g1 * x_mix
        hx2 = rmsnorm(x, norm2_g[l]) * (1.0 + sc2) + sh2
        x = x + g2 * conv_glu_ffn(hx2, w_up[l], ffn_conv_w[l], ffn_conv_b[l], w_down[l])

        if not last:
            na_qc, mla_qc = q_side(pc, na_q_g[l], mla_q_g[l])
            yac = full_attention(na_qc, na_kc, na_vc)
            ybc = full_attention(mla_qc, mla_kc, mla_vc)
            ycc = short_conv(pc, conv_w[l])
            ctx_mix = merge_branches(yac, ybc, ycc, pc[..., OFF_GATE:], w_br_a[l], w_br_b[l], w_br_c[l], w_o[l])
            ctx = ctx + cg1 * ctx_mix
            hc2 = rmsnorm(ctx, norm2_g[l]) * (1.0 + csc2) + csh2
            ctx = ctx + cg2 * conv_glu_ffn(hc2, w_up[l], ffn_conv_w[l], ffn_conv_b[l], w_down[l])
    return x
```

```python
import functools

import jax
import jax.numpy as jnp
from jax import lax
from jax.experimental import pallas as pl
from jax.experimental.pallas import tpu as pltpu

F32 = jnp.float32
BF16 = jnp.bfloat16

D_MODEL = 2048
GRID_W = 64
EPS = 1e-6
NA_HEADS = 8
NA_HEAD_DIM = 64
NA_WIN_H = 8
NA_WIN_W = 16
NA_W = NA_HEADS * NA_HEAD_DIM
MLA_HEADS = 4
MLA_NOPE = 128
MLA_ROPE = 64
MLA_V = 128
MLA_KV_RANK = 256
MLA_QK = MLA_NOPE + MLA_ROPE
MLA_QPAD = 256
ROPE_NFREQ = MLA_ROPE // 4
ROPE_BASE = 10000.0
CONV_CH = 512
D_FF = 5632

R_NA_K = 0
R_NA_V = R_NA_K + NA_W
R_CKV = R_NA_V + NA_W
R_KROPE = R_CKV + MLA_KV_RANK
R_NA_Q = R_KROPE + MLA_ROPE
R_MLA_Q = R_NA_Q + NA_W
R_CONV_U = R_MLA_Q + MLA_HEADS * MLA_QK
R_GATE = R_CONV_U + 3 * CONV_CH

P_NA_K = 0
P_NA_V = 512
P_NA_Q = 1024
P_MLA_Q = 1536
P_CKV = 2560
P_KROPE = 2816
P_QKV_END = 3072
P_UBC = 3072
P_COLS = 4608
G_COLS = 3 * D_MODEL

NA_QROWS = 4
NA_KROWS = 12
NA_TQ = NA_QROWS * GRID_W
NA_TK = NA_KROWS * GRID_W
NEG = -1e30

VMEM_LIMIT = 56 * 1024 * 1024


def _cparams(sem):
    return pltpu.CompilerParams(dimension_semantics=sem, vmem_limit_bytes=VMEM_LIMIT)


def _const_spec(shape):
    nd = len(shape)
    return pl.BlockSpec(shape, lambda *_: (0,) * nd, pipeline_mode=pl.Buffered(1))


MOD_KC = 256


def _mod_kernel(c_ref, w_ref, b_ref, o_ref):
    tn = o_ref.shape[1]
    nk = w_ref.shape[0] // MOD_KC

    def body(kc, acc):
        k0 = pl.multiple_of(kc * MOD_KC, MOD_KC)
        w = w_ref[pl.ds(k0, MOD_KC), :]
        c = c_ref[pl.ds(k0, MOD_KC), :]
        s = c * jax.nn.sigmoid(c)
        out = []
        for r in range(2):
            prod = s[:, r:r + 1] * w
            out.append(acc[r] + jnp.sum(prod.reshape(MOD_KC // 8, 8, tn), axis=0))
        return tuple(out)

    z = jnp.zeros((8, tn), F32)
    acc = lax.fori_loop(0, nk, body, (z, z))
    r0, r1 = [jnp.sum(a, axis=0, keepdims=True) + b_ref[...] for a in acc]
    ridx = lax.broadcasted_iota(jnp.int32, (8, tn), 0)
    o_ref[...] = jnp.where(ridx == 0, r0, jnp.where(ridx == 1, r1, 0.0))


def _modulation(c_cols, w_mod_l, b_mod_l):
    d, n = w_mod_l.shape
    tn = 1536
    return pl.pallas_call(
        _mod_kernel,
        grid=(n // tn,),
        in_specs=[
            _const_spec((d, 128)),
            pl.BlockSpec((d, tn), lambda j: (0, j)),
            pl.BlockSpec((1, tn), lambda j: (0, j)),
        ],
        out_specs=pl.BlockSpec((8, tn), lambda j: (0, j)),
        out_shape=jax.ShapeDtypeStruct((8, n), F32),
        compiler_params=_cparams(("arbitrary",)),
        name="modulation",
    )(c_cols, w_mod_l, b_mod_l.reshape(1, n))


NORM_RC = 128


def _norm_rows(x_ref, g_ref, sh_ref, sc_ref, hx_ref, row):
    tm = x_ref.shape[0]
    g = g_ref[...]
    sc1 = 1.0 + sc_ref[row:row + 1, :]
    sh = sh_ref[row:row + 1, :]

    def body(r, carry):
        r0 = pl.multiple_of(r * NORM_RC, NORM_RC)
        x = x_ref[pl.ds(r0, NORM_RC), :]
        ms = jnp.mean(x * x, axis=-1, keepdims=True)
        y = (x * lax.rsqrt(ms + EPS)) * g
        hx_ref[pl.ds(r0, NORM_RC), :] = (y * sc1 + sh).astype(BF16)
        return carry

    lax.fori_loop(0, tm // NORM_RC, body, 0)


def _in_proj_kernel(x_ref, g_ref, sh_ref, sc_ref, w_ref, p_ref, gate_ref, hx_ref, *, row, n_p):
    j = pl.program_id(1)

    @pl.when(j == 0)
    def _():
        _norm_rows(x_ref, g_ref, sh_ref, sc_ref, hx_ref, row)

    acc = jnp.dot(hx_ref[...], w_ref[...], preferred_element_type=F32)

    @pl.when(j < n_p)
    def _():
        p_ref[...] = acc.astype(BF16)

    @pl.when(j >= n_p)
    def _():
        gate_ref[...] = acc.astype(BF16)


def _in_proj(x2d, norm_g, mod, sh_blk, row, w_bf16, tm):
    n, d = x2d.shape
    tn = 1536
    n_p = P_COLS // tn
    n_g = G_COLS // tn
    return pl.pallas_call(
        functools.partial(_in_proj_kernel, row=row, n_p=n_p),
        grid=(n // tm, n_p + n_g),
        in_specs=[
            pl.BlockSpec((tm, d), lambda i, j: (i, 0)),
            _const_spec((1, d)),
            pl.BlockSpec((8, d), lambda i, j: (0, sh_blk), pipeline_mode=pl.Buffered(1)),
            pl.BlockSpec((8, d), lambda i, j: (0, sh_blk + 1), pipeline_mode=pl.Buffered(1)),
            pl.BlockSpec((d, tn), lambda i, j: (0, j)),
        ],
        out_specs=[
            pl.BlockSpec((tm, tn), lambda i, j: (i, jnp.minimum(j, n_p - 1))),
            pl.BlockSpec((tm, tn), lambda i, j: (i, jnp.maximum(j - n_p, 0))),
        ],
        out_shape=[
            jax.ShapeDtypeStruct((n, P_COLS), BF16),
            jax.ShapeDtypeStruct((n, G_COLS), BF16),
        ],
        scratch_shapes=[pltpu.VMEM((tm, d), BF16)],
        compiler_params=_cparams(("parallel", "arbitrary")),
        name="in_proj",
    )(x2d, norm_g.reshape(1, d), mod, mod, w_bf16)


def _ffn_up_kernel(x_ref, g_ref, sh_ref, sc_ref, w_ref, o_ref, hx_ref, *, row):
    @pl.when(pl.program_id(1) == 0)
    def _():
        _norm_rows(x_ref, g_ref, sh_ref, sc_ref, hx_ref, row)

    o_ref[...] = jnp.dot(hx_ref[...], w_ref[...], preferred_element_type=F32).astype(BF16)


def _ffn_up(x2d, norm_g, mod, sh_blk, row, w_bf16, tm):
    n, d = x2d.shape
    ncol = w_bf16.shape[1]
    tn = 1024
    return pl.pallas_call(
        functools.partial(_ffn_up_kernel, row=row),
        grid=(n // tm, ncol // tn),
        in_specs=[
            pl.BlockSpec((tm, d), lambda i, j: (i, 0)),
            _const_spec((1, d)),
            pl.BlockSpec((8, d), lambda i, j: (0, sh_blk), pipeline_mode=pl.Buffered(1)),
            pl.BlockSpec((8, d), lambda i, j: (0, sh_blk + 1), pipeline_mode=pl.Buffered(1)),
            pl.BlockSpec((d, tn), lambda i, j: (0, j)),
        ],
        out_specs=pl.BlockSpec((tm, tn), lambda i, j: (i, j)),
        out_shape=jax.ShapeDtypeStruct((n, ncol), BF16),
        scratch_shapes=[pltpu.VMEM((tm, d), BF16)],
        compiler_params=_cparams(("parallel", "arbitrary")),
        name="ffn_up",
    )(x2d, norm_g.reshape(1, d), mod, mod, w_bf16)


def _rope_rows(r, cos_ref, sin_ref):
    out = []
    for part in range(2):
        cs = cos_ref[part]
        sn = sin_ref[part]
        x1 = r[32 * part:32 * part + 16]
        x2 = r[32 * part + 16:32 * part + 32]
        out.append(x1 * cs - x2 * sn)
        out.append(x2 * cs + x1 * sn)
    return out


def _prep_kernel(p_ref, cos_ref, sin_ref, naqg_ref, nakg_ref, kvg_ref, mqg_ref, mkg_ref,
                 wkn_ref, wv_ref,
                 naqT_ref, nak_ref, navT_ref, mqT_ref, mk_ref, mvT_ref):
    tm = p_ref.shape[0]

    def head_norm_T(cols, g_ref, scale):
        xt = p_ref[:, cols:cols + NA_W].astype(F32).T
        x3 = xt.reshape(NA_HEADS, NA_HEAD_DIM, tm)
        ms = jnp.mean(x3 * x3, axis=1, keepdims=True)
        y = x3 * lax.rsqrt(ms + EPS) * g_ref[...][None]
        if scale != 1.0:
            y = y * scale
        return y.reshape(NA_W, tm)

    naqT_ref[...] = head_norm_T(P_NA_Q, naqg_ref, NA_HEAD_DIM ** -0.5).astype(BF16)
    nak_ref[...] = head_norm_T(P_NA_K, nakg_ref, 1.0).T.astype(BF16)
    navT_ref[...] = p_ref[:, P_NA_V:P_NA_V + NA_W].astype(F32).T.astype(BF16)

    mqg = mqg_ref[...]
    qscale = MLA_QK ** -0.5
    for h in range(MLA_HEADS):
        c0 = P_MLA_Q + h * MLA_QPAD
        qt = p_ref[:, c0:c0 + MLA_QPAD].astype(F32).T
        ms = jnp.sum(qt * qt, axis=0, keepdims=True) * (1.0 / MLA_QK)
        y = qt * lax.rsqrt(ms + EPS) * mqg
        r0 = h * MLA_QPAD
        mqT_ref[r0:r0 + MLA_NOPE, :] = (y[:MLA_NOPE] * qscale).astype(BF16)
        parts = _rope_rows(y[MLA_NOPE:MLA_QK], cos_ref, sin_ref)
        for k, part in enumerate(parts):
            a = r0 + MLA_NOPE + 16 * k
            mqT_ref[a:a + 16, :] = (part * qscale).astype(BF16)
        mqT_ref[r0 + MLA_QK:r0 + MLA_QPAD, :] = jnp.zeros((MLA_QPAD - MLA_QK, tm), BF16)

    ct = p_ref[:, P_CKV:P_CKV + MLA_KV_RANK].astype(F32).T
    ms = jnp.mean(ct * ct, axis=0, keepdims=True)
    cn = (ct * lax.rsqrt(ms + EPS) * kvg_ref[...]).astype(BF16)
    mvT_ref[...] = jnp.dot(wv_ref[...], cn, preferred_element_type=F32).astype(BF16)
    knT = jnp.dot(wkn_ref[...], cn, preferred_element_type=F32)
    krT = p_ref[:, P_KROPE:P_KROPE + 128].astype(F32).T[:MLA_ROPE]
    kr_ss = jnp.sum(krT * krT, axis=0, keepdims=True)
    mkg = mkg_ref[...]
    for h in range(MLA_HEADS):
        kn = knT[h * MLA_NOPE:(h + 1) * MLA_NOPE]
        ms = (jnp.sum(kn * kn, axis=0, keepdims=True) + kr_ss) * (1.0 / MLA_QK)
        rinv = lax.rsqrt(ms + EPS)
        kr = krT * rinv * mkg[MLA_NOPE:MLA_QK]
        pieces = [kn * rinv * mkg[:MLA_NOPE]] + _rope_rows(kr, cos_ref, sin_ref)
        pieces.append(jnp.zeros((MLA_QPAD - MLA_QK, tm), F32))
        kT = jnp.concatenate(pieces, axis=0)
        mk_ref[:, h * MLA_QPAD:(h + 1) * MLA_QPAD] = kT.T.astype(BF16)


def _prep(p, cosT, sinT, gains, wknT, wvT, tm):
    n = p.shape[0]
    naqg, nakg, kvg, mqg, mkg = gains
    return pl.pallas_call(
        _prep_kernel,
        grid=(n // tm,),
        in_specs=[
            pl.BlockSpec((tm, P_QKV_END), lambda i: (i, 0)),
            pl.BlockSpec((2, ROPE_NFREQ, tm), lambda i: (0, 0, i)),
            pl.BlockSpec((2, ROPE_NFREQ, tm), lambda i: (0, 0, i)),
            _const_spec((NA_HEAD_DIM, 1)),
            _const_spec((NA_HEAD_DIM, 1)),
            _const_spec((MLA_KV_RANK, 1)),
            _const_spec((MLA_QPAD, 1)),
            _const_spec((MLA_QPAD, 1)),
            _const_spec((MLA_HEADS * MLA_NOPE, MLA_KV_RANK)),
            _const_spec((MLA_HEADS * MLA_V, MLA_KV_RANK)),
        ],
        out_specs=[
            pl.BlockSpec((NA_W, tm), lambda i: (0, i)),
            pl.BlockSpec((tm, NA_W), lambda i: (i, 0)),
            pl.BlockSpec((NA_W, tm), lambda i: (0, i)),
            pl.BlockSpec((MLA_HEADS * MLA_QPAD, tm), lambda i: (0, i)),
            pl.BlockSpec((tm, MLA_HEADS * MLA_QPAD), lambda i: (i, 0)),
            pl.BlockSpec((MLA_HEADS * MLA_V, tm), lambda i: (0, i)),
        ],
        out_shape=[
            jax.ShapeDtypeStruct((NA_W, n), BF16),
            jax.ShapeDtypeStruct((n, NA_W), BF16),
            jax.ShapeDtypeStruct((NA_W, n), BF16),
            jax.ShapeDtypeStruct((MLA_HEADS * MLA_QPAD, n), BF16),
            jax.ShapeDtypeStruct((n, MLA_HEADS * MLA_QPAD), BF16),
            jax.ShapeDtypeStruct((MLA_HEADS * MLA_V, n), BF16),
        ],
        compiler_params=_cparams(("parallel",)),
        name="qkv_prep",
    )(p, cosT, sinT, naqg, nakg, kvg, mqg, mkg, wknT, wvT)


def _na_heads(q_ref, k_refs, vT_refs, bias_ref, o_ref):
    tq = q_ref.shape[1]
    upper = lax.broadcasted_iota(jnp.int32, (128, tq), 0) >= NA_HEAD_DIM
    outs = []
    for h in range(NA_HEADS):
        j, e = divmod(h, 2)
        q2 = q_ref[128 * j:128 * j + 128, :]
        qm = jnp.where(upper == bool(e), q2, jnp.zeros_like(q2))
        s_list = []
        off = 0
        for b, k_ref in enumerate(k_refs):
            s = jnp.dot(k_ref[:, 128 * j:128 * j + 128], qm, preferred_element_type=F32)
            nk = k_ref.shape[0]
            if bias_ref is not None and b < len(k_refs) - 1:
                s = s + bias_ref[0, h, off:off + nk, :]
            off += nk
            s_list.append(s)
        m = s_list[0].max(axis=0, keepdims=True)
        for s in s_list[1:]:
            m = jnp.maximum(m, s.max(axis=0, keepdims=True))
        l = jnp.zeros((1, tq), F32)
        acc = jnp.zeros((NA_HEAD_DIM, tq), F32)
        for s, vT_ref in zip(s_list, vT_refs):
            p = jnp.exp(s - m)
            l = l + p.sum(axis=0, keepdims=True)
            acc = acc + jnp.dot(vT_ref[NA_HEAD_DIM * h:NA_HEAD_DIM * (h + 1), :], p.astype(BF16),
                                preferred_element_type=F32)
        outs.append(acc / l)
    o_ref[...] = jnp.concatenate(outs, axis=0).T.astype(BF16)


def _na_latent_kernel(q_ref, k0, k1, k2, kc, v0, v1, v2, vc, bias_ref, o_ref):
    _na_heads(q_ref, [k0, k1, k2, kc], [v0, v1, v2, vc], bias_ref, o_ref)


def _na_ctx_kernel(q_ref, kc, vc, o_ref):
    _na_heads(q_ref, [kc], [vc], None, o_ref)


def _na_latent(naqT, nak, navT, nakc, navcT, bias):
    n = naqT.shape[1]
    nb = n // NA_TQ
    nctx = nakc.shape[0]

    def kb(o):
        return lambda i: (jnp.clip(i - 1, 0, nb - 3) + o, 0)

    def vb(o):
        return lambda i: (0, jnp.clip(i - 1, 0, nb - 3) + o)

    def variant(i):
        return (jnp.where(i == 0, 0, jnp.where(i == nb - 1, 2, 1)), 0, 0, 0)

    return pl.pallas_call(
        _na_latent_kernel,
        grid=(nb,),
        in_specs=[
            pl.BlockSpec((NA_W, NA_TQ), lambda i: (0, i)),
            pl.BlockSpec((NA_TQ, NA_W), kb(0)),
            pl.BlockSpec((NA_TQ, NA_W), kb(1)),
            pl.BlockSpec((NA_TQ, NA_W), kb(2)),
            _const_spec((nctx, NA_W)),
            pl.BlockSpec((NA_W, NA_TQ), vb(0)),
            pl.BlockSpec((NA_W, NA_TQ), vb(1)),
            pl.BlockSpec((NA_W, NA_TQ), vb(2)),
            _const_spec((NA_W, nctx)),
            pl.BlockSpec((1, NA_HEADS, NA_TK, NA_TQ), variant),
        ],
        out_specs=pl.BlockSpec((NA_TQ, NA_W), lambda i: (i, 0)),
        out_shape=jax.ShapeDtypeStruct((n, NA_W), BF16),
        compiler_params=_cparams(("parallel",)),
        name="na_latent",
    )(naqT, nak, nak, nak, nakc, navT, navT, navT, navcT, bias)


def _na_ctx(naqcT, nakc, navcT):
    nctx = nakc.shape[0]
    return pl.pallas_call(
        _na_ctx_kernel,
        grid=(1,),
        in_specs=[_const_spec((NA_W, nctx)), _const_spec((nctx, NA_W)), _const_spec((NA_W, nctx))],
        out_specs=pl.BlockSpec((nctx, NA_W), lambda i: (0, 0)),
        out_shape=jax.ShapeDtypeStruct((nctx, NA_W), BF16),
        compiler_params=_cparams(("arbitrary",)),
        name="na_ctx",
    )(naqcT, nakc, navcT)


def _na_bias_tables(rel_bias, rows):
    ql = jnp.arange(NA_TQ)
    kl = jnp.arange(NA_TK)
    qc = ql % GRID_W
    kc = kl % GRID_W
    start_c = jnp.clip(qc - NA_WIN_W // 2, 0, GRID_W - NA_WIN_W)
    in_col = (kc[:, None] >= start_c[None, :]) & (kc[:, None] < start_c[None, :] + NA_WIN_W)
    dc = kc[:, None] - qc[None, :] + (NA_WIN_W - 1)
    tabs = []
    for r0, base in ((0, 0), (NA_QROWS, 0), (rows - NA_QROWS, rows - NA_KROWS)):
        qr = r0 + ql // GRID_W
        kr = base + kl // GRID_W
        start_r = jnp.clip(qr - NA_WIN_H // 2, 0, rows - NA_WIN_H)
        in_row = (kr[:, None] >= start_r[None, :]) & (kr[:, None] < start_r[None, :] + NA_WIN_H)
        dr = kr[:, None] - qr[None, :] + (NA_WIN_H - 1)
        ok = in_row & in_col
        b = rel_bias[:, jnp.clip(dr, 0, 2 * NA_WIN_H - 2), jnp.clip(dc, 0, 2 * NA_WIN_W - 2)]
        tabs.append(jnp.where(ok[None], b.astype(F32), NEG))
    return jnp.stack(tabs, axis=0)


def _mla_kernel(*refs, tk, has_ctx):
    if has_ctx:
        qT_ref, k_ref, vT_ref, kc_ref, vcT_ref, o_ref, m_ref, l_ref, acc_ref = refs
    else:
        qT_ref, k_ref, vT_ref, o_ref, m_ref, l_ref, acc_ref = refs
    tq = qT_ref.shape[1]
    qT = qT_ref[...]
    nk = k_ref.shape[0] // tk

    if has_ctx:
        s = jnp.dot(kc_ref[...], qT, preferred_element_type=F32)
        m = s.max(axis=0, keepdims=True)
        p = jnp.exp(s - m)
        m_ref[...] = m
        l_ref[...] = p.sum(axis=0, keepdims=True)
        acc_ref[...] = jnp.dot(vcT_ref[...], p.astype(BF16), preferred_element_type=F32)
    else:
        m_ref[...] = jnp.full((1, tq), NEG, F32)
        l_ref[...] = jnp.zeros((1, tq), F32)
        acc_ref[...] = jnp.zeros((MLA_V, tq), F32)

    def body(kk, carry):
        k0 = pl.multiple_of(kk * tk, tk)
        s = jnp.dot(k_ref[pl.ds(k0, tk), :], qT, preferred_element_type=F32)
        m_old = m_ref[...]
        m_new = jnp.maximum(m_old, s.max(axis=0, keepdims=True))
        alpha = jnp.exp(m_old - m_new)
        p = jnp.exp(s - m_new)
        l_ref[...] = alpha * l_ref[...] + p.sum(axis=0, keepdims=True)
        acc_ref[...] = alpha * acc_ref[...] + jnp.dot(vT_ref[:, pl.ds(k0, tk)], p.astype(BF16),
                                                      preferred_element_type=F32)
        m_ref[...] = m_new
        return carry

    lax.fori_loop(0, nk, body, 0)
    o_ref[...] = (acc_ref[...] / l_ref[...]).T.astype(BF16)


def _mla_attention(mqT, mk, mvT, ctx_kv, tq, tk):
    nq = mqT.shape[1]
    nkeys = mk.shape[0]
    has_ctx = ctx_kv is not None
    in_specs = [
        pl.BlockSpec((MLA_QPAD, tq), lambda h, i: (h, i)),
        pl.BlockSpec((nkeys, MLA_QPAD), lambda h, i: (0, h)),
        pl.BlockSpec((MLA_V, nkeys), lambda h, i: (h, 0)),
    ]
    args = [mqT, mk, mvT]
    if has_ctx:
        kc, vcT = ctx_kv
        nctx = kc.shape[0]
        in_specs += [
            pl.BlockSpec((nctx, MLA_QPAD), lambda h, i: (0, h)),
            pl.BlockSpec((MLA_V, nctx), lambda h, i: (h, 0)),
        ]
        args += [kc, vcT]
    return pl.pallas_call(
        functools.partial(_mla_kernel, tk=tk, has_ctx=has_ctx),
        grid=(MLA_HEADS, nq // tq),
        in_specs=in_specs,
        out_specs=pl.BlockSpec((tq, MLA_V), lambda h, i: (i, h)),
        out_shape=jax.ShapeDtypeStruct((nq, MLA_HEADS * MLA_V), BF16),
        scratch_shapes=[pltpu.VMEM((1, tq), F32), pltpu.VMEM((1, tq), F32), pltpu.VMEM((MLA_V, tq), F32)],
        compiler_params=_cparams(("parallel", "arbitrary")),
        name="mla_attention",
    )(*args)


def _conv3_rows(x, prev_row, next_row, w_ref):
    tm = x.shape[0]
    rows = lax.broadcasted_iota(jnp.int32, (tm, 1), 0)
    xm1 = jnp.where(rows == 0, prev_row, pltpu.roll(x, 1, 0))
    xp1 = jnp.where(rows == tm - 1, next_row, pltpu.roll(x, tm - 1, 0))
    return w_ref[0:1, :] * xm1 + w_ref[1:2, :] * x + w_ref[2:3, :] * xp1


HALO = 16


def _merge_kernel(ya_ref, yb_ref, ubc_ref, prev_ref, next_ref, gate_ref, x_ref, g1_ref, cw_ref,
                  wa_ref, wb_ref, wc_ref, wo_ref, o_ref, *, row):
    i = pl.program_id(0)
    nblk = pl.num_programs(0)
    ubc = ubc_ref[...].astype(F32)
    u = ubc[:, :CONV_CH]
    gb = ubc[:, CONV_CH:2 * CONV_CH]
    gc = ubc[:, 2 * CONV_CH:]
    cu = gc * u
    pr = prev_ref[...].astype(F32)[HALO - 1:HALO, :]
    nx = next_ref[...].astype(F32)[0:1, :]
    cu_prev = pr[:, 2 * CONV_CH:] * pr[:, :CONV_CH] * (i > 0).astype(F32)
    cu_next = nx[:, 2 * CONV_CH:] * nx[:, :CONV_CH] * (i < nblk - 1).astype(F32)
    yc = (gb * _conv3_rows(cu, cu_prev, cu_next, cw_ref)).astype(BF16)

    a = jnp.dot(ya_ref[...], wa_ref[...], preferred_element_type=F32)
    b = jnp.dot(yb_ref[...], wb_ref[...], preferred_element_type=F32)
    c = jnp.dot(yc, wc_ref[...], preferred_element_type=F32)
    d = D_MODEL
    merged = (jax.nn.sigmoid(gate_ref[:, :d].astype(F32)) * a
              + jax.nn.sigmoid(gate_ref[:, d:2 * d].astype(F32)) * b
              + jax.nn.sigmoid(gate_ref[:, 2 * d:].astype(F32)) * c)
    mix = jnp.dot(merged.astype(BF16), wo_ref[...], preferred_element_type=F32)
    o_ref[...] = x_ref[...] + g1_ref[row:row + 1, :] * mix


def _merge(ya, yb, p, gates, x2d, mod, g1_blk, row, conv_w8, wa, wb, wc, wo, tm):
    n, d = x2d.shape
    ubc_blk = P_UBC // (3 * CONV_CH)
    hb = tm // HALO
    last = n // HALO - 1
    return pl.pallas_call(
        functools.partial(_merge_kernel, row=row),
        grid=(n // tm,),
        in_specs=[
            pl.BlockSpec((tm, NA_W), lambda i: (i, 0)),
            pl.BlockSpec((tm, MLA_HEADS * MLA_V), lambda i: (i, 0)),
            pl.BlockSpec((tm, 3 * CONV_CH), lambda i: (i, ubc_blk)),
            pl.BlockSpec((HALO, 3 * CONV_CH), lambda i: (jnp.maximum(i * hb - 1, 0), ubc_blk)),
            pl.BlockSpec((HALO, 3 * CONV_CH), lambda i: (jnp.minimum((i + 1) * hb, last), ubc_blk)),
            pl.BlockSpec((tm, G_COLS), lambda i: (i, 0)),
            pl.BlockSpec((tm, d), lambda i: (i, 0)),
            pl.BlockSpec((8, d), lambda i: (0, g1_blk), pipeline_mode=pl.Buffered(1)),
            _const_spec((8, CONV_CH)),
            _const_spec(wa.shape),
            _const_spec(wb.shape),
            _const_spec(wc.shape),
            _const_spec(wo.shape),
        ],
        out_specs=pl.BlockSpec((tm, d), lambda i: (i, 0)),
        out_shape=jax.ShapeDtypeStruct((n, d), F32),
        compiler_params=_cparams(("parallel",)),
        name="merge_out",
    )(ya, yb, p, p, p, gates, x2d, mod, conv_w8, wa, wb, wc, wo)


FFN_KC = D_FF // 4


def _ffn_down_kernel(a_ref, prev_ref, next_ref, val_ref, cw_ref, wd_ref, x_ref, g2_ref, o_ref, acc_ref, *, row):
    i = pl.program_id(0)
    kk = pl.program_id(1)
    nblk = pl.num_programs(0)
    a = a_ref[...].astype(F32)
    pr = prev_ref[...].astype(F32)[HALO - 1:HALO, :] * (i > 0).astype(F32)
    nx = next_ref[...].astype(F32)[0:1, :] * (i < nblk - 1).astype(F32)
    a = _conv3_rows(a, pr, nx, cw_ref) + cw_ref[3:4, :]
    h = (a * jax.nn.sigmoid(a) * val_ref[...].astype(F32)).astype(BF16)
    part = jnp.dot(h, wd_ref[...], preferred_element_type=F32)

    @pl.when(kk == 0)
    def _():
        acc_ref[...] = part

    @pl.when(kk > 0)
    def _():
        acc_ref[...] += part

    @pl.when(kk == pl.num_programs(1) - 1)
    def _():
        o_ref[...] = x_ref[...] + g2_ref[row:row + 1, :] * acc_ref[...]


def _ffn_down(av, x2d, mod, g2_blk, row, cw8, wd, tm):
    n, d = x2d.shape
    nkc = D_FF // FFN_KC
    hb = tm // HALO
    last = n // HALO - 1
    return pl.pallas_call(
        functools.partial(_ffn_down_kernel, row=row),
        grid=(n // tm, nkc),
        in_specs=[
            pl.BlockSpec((tm, FFN_KC), lambda i, k: (i, k)),
            pl.BlockSpec((HALO, FFN_KC), lambda i, k: (jnp.maximum(i * hb - 1, 0), k)),
            pl.BlockSpec((HALO, FFN_KC), lambda i, k: (jnp.minimum((i + 1) * hb, last), k)),
            pl.BlockSpec((tm, FFN_KC), lambda i, k: (i, nkc + k)),
            pl.BlockSpec((8, FFN_KC), lambda i, k: (0, k)),
            pl.BlockSpec((FFN_KC, d), lambda i, k: (k, 0)),
            pl.BlockSpec((tm, d), lambda i, k: (i, 0)),
            pl.BlockSpec((8, d), lambda i, k: (0, g2_blk), pipeline_mode=pl.Buffered(1)),
        ],
        out_specs=pl.BlockSpec((tm, d), lambda i, k: (i, 0)),
        out_shape=jax.ShapeDtypeStruct((n, d), F32),
        scratch_shapes=[pltpu.VMEM((tm, d), F32)],
        compiler_params=_cparams(("parallel", "arbitrary")),
        name="ffn_down",
    )(av, av, av, av, cw8, wd, x2d, mod)


def _layout_w_in(w):
    d = w.shape[0]
    z64 = jnp.zeros((d, 64), w.dtype)
    cols = [w[:, R_NA_K:R_NA_V], w[:, R_NA_V:R_CKV], w[:, R_NA_Q:R_MLA_Q]]
    for h in range(MLA_HEADS):
        cols += [w[:, R_MLA_Q + h * MLA_QK:R_MLA_Q + (h + 1) * MLA_QK], z64]
    cols += [w[:, R_CKV:R_KROPE], w[:, R_KROPE:R_NA_Q], z64, z64, z64,
             w[:, R_CONV_U:R_GATE], w[:, R_GATE:]]
    return jnp.concatenate(cols, axis=1).astype(BF16)


def _pad_rows(a, rows):
    return jnp.concatenate([a, jnp.zeros((rows - a.shape[0],) + a.shape[1:], a.dtype)], axis=0)


def _rope_tables_T(n_tok):
    t = jnp.arange(n_tok)
    row = (t // GRID_W).astype(F32)
    col = (t % GRID_W).astype(F32)
    inv = ROPE_BASE ** (-jnp.arange(ROPE_NFREQ, dtype=F32) / ROPE_NFREQ)
    ang = jnp.stack([inv[:, None] * row[None, :], inv[:, None] * col[None, :]], axis=0)
    return jnp.cos(ang), jnp.sin(ang)


def kernel(x, c, ctx, c_ctx, w_mod, b_mod, norm1_g, w_in, na_q_g, na_k_g, na_rel_bias, mla_kv_g, w_kv_up,
           mla_q_g, mla_k_g, conv_w, w_br_a, w_br_b, w_br_c, w_o, norm2_g, w_up, ffn_conv_w, ffn_conv_b, w_down):
    bsz, seq, d = x.shape
    nctx = ctx.shape[1]
    depth = w_mod.shape[0]
    rows = seq // GRID_W
    assert bsz == 1 and d == D_MODEL and seq % NA_TQ == 0 and rows >= NA_KROWS and nctx % 256 == 0

    xs = x[0]
    cs = ctx[0]
    tm_x = min(1024, seq)
    tm_half = min(512, seq)
    tm_merge = min(256, seq)
    tq_mla = min(512, seq)
    tk_mla = min(512, seq)

    cos_x, sin_x = _rope_tables_T(seq)
    cos_c = jnp.ones((2, ROPE_NFREQ, nctx), F32)
    sin_c = jnp.zeros((2, ROPE_NFREQ, nctx), F32)
    c_cols = jnp.zeros((d, 128), F32).at[:, 0].set(c[0]).at[:, 1].set(c_ctx)

    for l in range(depth):
        last = l == depth - 1
        mod = _modulation(c_cols, w_mod[l], b_mod[l])

        w_in_l = _layout_w_in(w_in[l])
        kv_up = w_kv_up[l].reshape(MLA_KV_RANK, MLA_HEADS, MLA_NOPE + MLA_V)
        wknT = kv_up[:, :, :MLA_NOPE].reshape(MLA_KV_RANK, -1).T.astype(BF16)
        wvT = kv_up[:, :, MLA_NOPE:].reshape(MLA_KV_RANK, -1).T.astype(BF16)
        gains = (na_q_g[l].reshape(-1, 1), na_k_g[l].reshape(-1, 1), mla_kv_g[l].reshape(-1, 1),
                 _pad_rows(mla_q_g[l].reshape(-1, 1), MLA_QPAD), _pad_rows(mla_k_g[l].reshape(-1, 1), MLA_QPAD))
        bias = _na_bias_tables(na_rel_bias[l], rows)
        conv_w8 = _pad_rows(conv_w[l], 8)
        ffn_cw8 = _pad_rows(jnp.concatenate([ffn_conv_w[l], ffn_conv_b[l][None]], axis=0), 8)
        wa, wb, wc, wo = (w.astype(BF16) for w in (w_br_a[l], w_br_b[l], w_br_c[l], w_o[l]))
        w_up_l = w_up[l].astype(BF16)
        w_down_l = w_down[l].astype(BF16)

        pc, gc = _in_proj(cs, norm1_g[l], mod, 0, 1, w_in_l, nctx)
        naqcT, nakc, navcT, mqcT, mkc, mvcT = _prep(pc, cos_c, sin_c, gains, wknT, wvT, nctx)
        px, gx = _in_proj(xs, norm1_g[l], mod, 0, 0, w_in_l, tm_x)
        naqT, nak, navT, mqT, mk, mvT = _prep(px, cos_x, sin_x, gains, wknT, wvT, tm_half)

        ya = _na_latent(naqT, nak, navT, nakc, navcT, bias)
        yb = _mla_attention(mqT, mk, mvT, (mkc, mvcT), tq_mla, tk_mla)
        xs = _merge(ya, yb, px, gx, xs, mod, 2, 0, conv_w8, wa, wb, wc, wo, tm_merge)
        av = _ffn_up(xs, norm2_g[l], mod, 3, 0, w_up_l, tm_x)
        xs = _ffn_down(av, xs, mod, 5, 0, ffn_cw8, w_down_l, tm_half)

        if not last:
            yac = _na_ctx(naqcT, nakc, navcT)
            ybc = _mla_attention(mqcT, mkc, mvcT, None, nctx, nctx)
            cs = _merge(yac, ybc, pc, gc, cs, mod, 2, 1, conv_w8, wa, wb, wc, wo, nctx)
            avc = _ffn_up(cs, norm2_g[l], mod, 3, 1, w_up_l, nctx)
            cs = _ffn_down(avc, cs, mod, 5, 1, ffn_cw8, w_down_l, nctx)

    return xs[None]
```

```python
import functools

import math

import jax
import jax.numpy as jnp
import numpy as np
from jax import lax
from jax.experimental import pallas as pl
from jax.experimental.pallas import tpu as pltpu

F32 = jnp.float32
BF16 = jnp.bfloat16

D_MODEL = 2048
GRID_W = 64
EPS = 1e-6
NA_HEADS = 8
NA_HEAD_DIM = 64
NA_WIN_H = 8
NA_WIN_W = 16
NA_W = NA_HEADS * NA_HEAD_DIM
MLA_HEADS = 4
MLA_NOPE = 128
MLA_ROPE = 64
MLA_V = 128
MLA_KV_RANK = 256
MLA_QK = MLA_NOPE + MLA_ROPE
MLA_QPAD = 256
ROPE_NFREQ = MLA_ROPE // 4
ROPE_BASE = 10000.0
CONV_CH = 512
D_FF = 5632

R_NA_K = 0
R_NA_V = R_NA_K + NA_W
R_CKV = R_NA_V + NA_W
R_KROPE = R_CKV + MLA_KV_RANK
R_NA_Q = R_KROPE + MLA_ROPE
R_MLA_Q = R_NA_Q + NA_W
R_CONV_U = R_MLA_Q + MLA_HEADS * MLA_QK
R_GATE = R_CONV_U + 3 * CONV_CH

P_NA_K = 0
P_NA_V = 512
P_NA_Q = 1024
P_MLA_Q = 1536
P_CKV = 2560
P_KROPE = 2816
P_QKV_END = 3072
P_UBC = 3072
P_COLS = 4608
G_COLS = 3 * D_MODEL

NA_QROWS = 4
NA_KROWS = 12
NA_TQ = NA_QROWS * GRID_W
NA_TK = NA_KROWS * GRID_W
NEG = -1e30
LOG2E = math.log2(math.e)

VMEM_LIMIT = 56 * 1024 * 1024


def _cparams(sem):
    return pltpu.CompilerParams(dimension_semantics=sem, vmem_limit_bytes=VMEM_LIMIT)


def _const_spec(shape):
    nd = len(shape)
    return pl.BlockSpec(shape, lambda *_: (0,) * nd, pipeline_mode=pl.Buffered(1))


MOD_KC = 256


def _mod_kernel(c_ref, w_ref, b_ref, o_ref):
    tn = o_ref.shape[1]
    nk = w_ref.shape[0] // MOD_KC

    def body(kc, acc):
        k0 = pl.multiple_of(kc * MOD_KC, MOD_KC)
        w = w_ref[pl.ds(k0, MOD_KC), :]
        c = c_ref[pl.ds(k0, MOD_KC), :]
        s = c * jax.nn.sigmoid(c)
        out = []
        for r in range(2):
            prod = s[:, r:r + 1] * w
            out.append(acc[r] + jnp.sum(prod.reshape(MOD_KC // 8, 8, tn), axis=0))
        return tuple(out)

    z = jnp.zeros((8, tn), F32)
    acc = lax.fori_loop(0, nk, body, (z, z))
    r0, r1 = [jnp.sum(a, axis=0, keepdims=True) + b_ref[...] for a in acc]
    ridx = lax.broadcasted_iota(jnp.int32, (8, tn), 0)
    o_ref[...] = jnp.where(ridx == 0, r0, jnp.where(ridx == 1, r1, 0.0))


def _modulation(c_cols, w_mod_l, b_mod_l):
    d, n = w_mod_l.shape
    tn = 1536
    return pl.pallas_call(
        _mod_kernel,
        grid=(n // tn,),
        in_specs=[
            _const_spec((d, 128)),
            pl.BlockSpec((d, tn), lambda j: (0, j)),
            pl.BlockSpec((1, tn), lambda j: (0, j)),
        ],
        out_specs=pl.BlockSpec((8, tn), lambda j: (0, j)),
        out_shape=jax.ShapeDtypeStruct((8, n), F32),
        compiler_params=_cparams(("arbitrary",)),
        name="modulation",
    )(c_cols, w_mod_l, b_mod_l.reshape(1, n))


NORM_RC = 128


def _norm_rows(x_ref, g_ref, sh_ref, sc_ref, hx_ref, row):
    tm = x_ref.shape[0]
    g = g_ref[...]
    sc1 = 1.0 + sc_ref[row:row + 1, :]
    sh = sh_ref[row:row + 1, :]

    def body(r, carry):
        r0 = pl.multiple_of(r * NORM_RC, NORM_RC)
        x = x_ref[pl.ds(r0, NORM_RC), :]
        ms = jnp.mean(x * x, axis=-1, keepdims=True)
        y = (x * lax.rsqrt(ms + EPS)) * g
        hx_ref[pl.ds(r0, NORM_RC), :] = (y * sc1 + sh).astype(BF16)
        return carry

    lax.fori_loop(0, tm // NORM_RC, body, 0)


def _in_proj_kernel(x_ref, g_ref, sh_ref, sc_ref, w_ref, p_ref, gate_ref, hx_ref, *, row, n_p):
    j = pl.program_id(1)

    @pl.when(j == 0)
    def _():
        _norm_rows(x_ref, g_ref, sh_ref, sc_ref, hx_ref, row)

    acc = jnp.dot(hx_ref[...], w_ref[...], preferred_element_type=F32)

    @pl.when(j < n_p)
    def _():
        p_ref[...] = acc.astype(BF16)

    @pl.when(j >= n_p)
    def _():
        gate_ref[...] = acc.astype(BF16)


def _in_proj(x2d, norm_g, mod, sh_blk, row, w_bf16, tm):
    n, d = x2d.shape
    tn = 1536
    n_p = P_COLS // tn
    n_g = G_COLS // tn
    return pl.pallas_call(
        functools.partial(_in_proj_kernel, row=row, n_p=n_p),
        grid=(n // tm, n_p + n_g),
        in_specs=[
            pl.BlockSpec((tm, d), lambda i, j: (i, 0)),
            _const_spec((1, d)),
            pl.BlockSpec((8, d), lambda i, j: (0, sh_blk), pipeline_mode=pl.Buffered(1)),
            pl.BlockSpec((8, d), lambda i, j: (0, sh_blk + 1), pipeline_mode=pl.Buffered(1)),
            pl.BlockSpec((d, tn), lambda i, j: (0, j)),
        ],
        out_specs=[
            pl.BlockSpec((tm, tn), lambda i, j: (i, jnp.minimum(j, n_p - 1))),
            pl.BlockSpec((tm, tn), lambda i, j: (i, jnp.maximum(j - n_p, 0))),
        ],
        out_shape=[
            jax.ShapeDtypeStruct((n, P_COLS), BF16),
            jax.ShapeDtypeStruct((n, G_COLS), BF16),
        ],
        scratch_shapes=[pltpu.VMEM((tm, d), BF16)],
        compiler_params=_cparams(("parallel", "arbitrary")),
        name="in_proj",
    )(x2d, norm_g.reshape(1, d), mod, mod, w_bf16)


def _ffn_up_kernel(x_ref, g_ref, sh_ref, sc_ref, w_ref, o_ref, hx_ref, *, row):
    @pl.when(pl.program_id(1) == 0)
    def _():
        _norm_rows(x_ref, g_ref, sh_ref, sc_ref, hx_ref, row)

    o_ref[...] = jnp.dot(hx_ref[...], w_ref[...], preferred_element_type=F32).astype(BF16)


def _ffn_up(x2d, norm_g, mod, sh_blk, row, w_bf16, tm):
    n, d = x2d.shape
    ncol = w_bf16.shape[1]
    tn = 1024
    return pl.pallas_call(
        functools.partial(_ffn_up_kernel, row=row),
        grid=(n // tm, ncol // tn),
        in_specs=[
            pl.BlockSpec((tm, d), lambda i, j: (i, 0)),
            _const_spec((1, d)),
            pl.BlockSpec((8, d), lambda i, j: (0, sh_blk), pipeline_mode=pl.Buffered(1)),
            pl.BlockSpec((8, d), lambda i, j: (0, sh_blk + 1), pipeline_mode=pl.Buffered(1)),
            pl.BlockSpec((d, tn), lambda i, j: (0, j)),
        ],
        out_specs=pl.BlockSpec((tm, tn), lambda i, j: (i, j)),
        out_shape=jax.ShapeDtypeStruct((n, ncol), BF16),
        scratch_shapes=[pltpu.VMEM((tm, d), BF16)],
        compiler_params=_cparams(("parallel", "arbitrary")),
        name="ffn_up",
    )(x2d, norm_g.reshape(1, d), mod, mod, w_bf16)


def _rope_rows(r, cos_ref, sin_ref):
    out = []
    for part in range(2):
        cs = cos_ref[part]
        sn = sin_ref[part]
        x1 = r[32 * part:32 * part + 16]
        x2 = r[32 * part + 16:32 * part + 32]
        out.append(x1 * cs - x2 * sn)
        out.append(x2 * cs + x1 * sn)
    return out


def _prep_kernel(p_ref, cos_ref, sin_ref, naqg_ref, nakg_ref, kvg_ref, mqg_ref, mkg_ref,
                 wkn_ref, wv_ref,
                 naqT_ref, nak_ref, navT_ref, mqT_ref, mk_ref, mvT_ref):
    tm = p_ref.shape[0]

    def head_norm_T(cols, g_ref, scale):
        xt = p_ref[:, cols:cols + NA_W].astype(F32).T
        x3 = xt.reshape(NA_HEADS, NA_HEAD_DIM, tm)
        ms = jnp.mean(x3 * x3, axis=1, keepdims=True)
        y = x3 * lax.rsqrt(ms + EPS) * g_ref[...][None]
        if scale != 1.0:
            y = y * scale
        return y.reshape(NA_W, tm)

    naqT_ref[...] = head_norm_T(P_NA_Q, naqg_ref, NA_HEAD_DIM ** -0.5 * LOG2E).astype(BF16)
    nak_ref[...] = head_norm_T(P_NA_K, nakg_ref, 1.0).T.astype(BF16)
    navT_ref[...] = p_ref[:, P_NA_V:P_NA_V + NA_W].astype(F32).T.astype(BF16)

    mqg = mqg_ref[...]
    qscale = MLA_QK ** -0.5 * LOG2E
    for h in range(MLA_HEADS):
        c0 = P_MLA_Q + h * MLA_QPAD
        qt = p_ref[:, c0:c0 + MLA_QPAD].astype(F32).T
        ms = jnp.sum(qt * qt, axis=0, keepdims=True) * (1.0 / MLA_QK)
        y = qt * lax.rsqrt(ms + EPS) * mqg
        r0 = h * MLA_QPAD
        mqT_ref[r0:r0 + MLA_NOPE, :] = (y[:MLA_NOPE] * qscale).astype(BF16)
        parts = _rope_rows(y[MLA_NOPE:MLA_QK], cos_ref, sin_ref)
        for k, part in enumerate(parts):
            a = r0 + MLA_NOPE + 16 * k
            mqT_ref[a:a + 16, :] = (part * qscale).astype(BF16)
        mqT_ref[r0 + MLA_QK:r0 + MLA_QPAD, :] = jnp.zeros((MLA_QPAD - MLA_QK, tm), BF16)

    ct = p_ref[:, P_CKV:P_CKV + MLA_KV_RANK].astype(F32).T
    ms = jnp.mean(ct * ct, axis=0, keepdims=True)
    cn = (ct * lax.rsqrt(ms + EPS) * kvg_ref[...]).astype(BF16)
    mvT_ref[...] = jnp.dot(wv_ref[...], cn, preferred_element_type=F32).astype(BF16)
    knT = jnp.dot(wkn_ref[...], cn, preferred_element_type=F32)
    krT = p_ref[:, P_KROPE:P_KROPE + 128].astype(F32).T[:MLA_ROPE]
    kr_ss = jnp.sum(krT * krT, axis=0, keepdims=True)
    mkg = mkg_ref[...]
    for h in range(MLA_HEADS):
        kn = knT[h * MLA_NOPE:(h + 1) * MLA_NOPE]
        ms = (jnp.sum(kn * kn, axis=0, keepdims=True) + kr_ss) * (1.0 / MLA_QK)
        rinv = lax.rsqrt(ms + EPS)
        kr = krT * rinv * mkg[MLA_NOPE:MLA_QK]
        pieces = [kn * rinv * mkg[:MLA_NOPE]] + _rope_rows(kr, cos_ref, sin_ref)
        pieces.append(jnp.zeros((MLA_QPAD - MLA_QK, tm), F32))
        kT = jnp.concatenate(pieces, axis=0)
        mk_ref[:, h * MLA_QPAD:(h + 1) * MLA_QPAD] = kT.T.astype(BF16)


def _prep(p, cosT, sinT, gains, wknT, wvT, tm):
    n = p.shape[0]
    naqg, nakg, kvg, mqg, mkg = gains
    return pl.pallas_call(
        _prep_kernel,
        grid=(n // tm,),
        in_specs=[
            pl.BlockSpec((tm, P_QKV_END), lambda i: (i, 0)),
            pl.BlockSpec((2, ROPE_NFREQ, tm), lambda i: (0, 0, i)),
            pl.BlockSpec((2, ROPE_NFREQ, tm), lambda i: (0, 0, i)),
            _const_spec((NA_HEAD_DIM, 1)),
            _const_spec((NA_HEAD_DIM, 1)),
            _const_spec((MLA_KV_RANK, 1)),
            _const_spec((MLA_QPAD, 1)),
            _const_spec((MLA_QPAD, 1)),
            _const_spec((MLA_HEADS * MLA_NOPE, MLA_KV_RANK)),
            _const_spec((MLA_HEADS * MLA_V, MLA_KV_RANK)),
        ],
        out_specs=[
            pl.BlockSpec((NA_W, tm), lambda i: (0, i)),
            pl.BlockSpec((tm, NA_W), lambda i: (i, 0)),
            pl.BlockSpec((NA_W, tm), lambda i: (0, i)),
            pl.BlockSpec((MLA_HEADS * MLA_QPAD, tm), lambda i: (0, i)),
            pl.BlockSpec((tm, MLA_HEADS * MLA_QPAD), lambda i: (i, 0)),
            pl.BlockSpec((MLA_HEADS * MLA_V, tm), lambda i: (0, i)),
        ],
        out_shape=[
            jax.ShapeDtypeStruct((NA_W, n), BF16),
            jax.ShapeDtypeStruct((n, NA_W), BF16),
            jax.ShapeDtypeStruct((NA_W, n), BF16),
            jax.ShapeDtypeStruct((MLA_HEADS * MLA_QPAD, n), BF16),
            jax.ShapeDtypeStruct((n, MLA_HEADS * MLA_QPAD), BF16),
            jax.ShapeDtypeStruct((MLA_HEADS * MLA_V, n), BF16),
        ],
        compiler_params=_cparams(("parallel",)),
        name="qkv_prep",
    )(p, cosT, sinT, naqg, nakg, kvg, mqg, mkg, wknT, wvT)


def _na_heads(q_ref, k_refs, vT_refs, bias_ref, o_ref):
    tq = q_ref.shape[1]
    upper = lax.broadcasted_iota(jnp.int32, (128, tq), 0) >= NA_HEAD_DIM
    outs = []
    for h in range(NA_HEADS):
        j, e = divmod(h, 2)
        q2 = q_ref[128 * j:128 * j + 128, :]
        qm = jnp.where(upper == bool(e), q2, jnp.zeros_like(q2))
        s_list = []
        off = 0
        for b, k_ref in enumerate(k_refs):
            s = jnp.dot(k_ref[:, 128 * j:128 * j + 128], qm, preferred_element_type=F32)
            nk = k_ref.shape[0]
            if bias_ref is not None and b < len(k_refs) - 1:
                s = s + bias_ref[0, h, off:off + nk, :]
            off += nk
            s_list.append(s)
        m = s_list[0].max(axis=0, keepdims=True)
        for s in s_list[1:]:
            m = jnp.maximum(m, s.max(axis=0, keepdims=True))
        l = jnp.zeros((1, tq), F32)
        acc = jnp.zeros((NA_HEAD_DIM, tq), F32)
        for s, vT_ref in zip(s_list, vT_refs):
            p = jnp.exp2(s - m)
            l = l + p.sum(axis=0, keepdims=True)
            acc = acc + jnp.dot(vT_ref[NA_HEAD_DIM * h:NA_HEAD_DIM * (h + 1), :], p.astype(BF16),
                                preferred_element_type=F32)
        outs.append(acc / l)
    o_ref[...] = jnp.concatenate(outs, axis=0).T.astype(BF16)


def _na_latent_kernel(q_ref, k0, k1, k2, kc, v0, v1, v2, vc, bias_ref, o_ref):
    _na_heads(q_ref, [k0, k1, k2, kc], [v0, v1, v2, vc], bias_ref, o_ref)


def _na_ctx_kernel(q_ref, kc, vc, o_ref):
    _na_heads(q_ref, [kc], [vc], None, o_ref)


def _na_latent(naqT, nak, navT, nakc, navcT, bias):
    n = naqT.shape[1]
    nb = n // NA_TQ
    nctx = nakc.shape[0]

    def kb(o):
        return lambda i: (jnp.clip(i - 1, 0, nb - 3) + o, 0)

    def vb(o):
        return lambda i: (0, jnp.clip(i - 1, 0, nb - 3) + o)

    def variant(i):
        return (jnp.where(i == 0, 0, jnp.where(i == nb - 1, 2, 1)), 0, 0, 0)

    return pl.pallas_call(
        _na_latent_kernel,
        grid=(nb,),
        in_specs=[
            pl.BlockSpec((NA_W, NA_TQ), lambda i: (0, i)),
            pl.BlockSpec((NA_TQ, NA_W), kb(0)),
            pl.BlockSpec((NA_TQ, NA_W), kb(1)),
            pl.BlockSpec((NA_TQ, NA_W), kb(2)),
            _const_spec((nctx, NA_W)),
            pl.BlockSpec((NA_W, NA_TQ), vb(0)),
            pl.BlockSpec((NA_W, NA_TQ), vb(1)),
            pl.BlockSpec((NA_W, NA_TQ), vb(2)),
            _const_spec((NA_W, nctx)),
            pl.BlockSpec((1, NA_HEADS, NA_TK, NA_TQ), variant),
        ],
        out_specs=pl.BlockSpec((NA_TQ, NA_W), lambda i: (i, 0)),
        out_shape=jax.ShapeDtypeStruct((n, NA_W), BF16),
        compiler_params=_cparams(("parallel",)),
        name="na_latent",
    )(naqT, nak, nak, nak, nakc, navT, navT, navT, navcT, bias)


def _na_ctx(naqcT, nakc, navcT):
    nctx = nakc.shape[0]
    return pl.pallas_call(
        _na_ctx_kernel,
        grid=(1,),
        in_specs=[_const_spec((NA_W, nctx)), _const_spec((nctx, NA_W)), _const_spec((NA_W, nctx))],
        out_specs=pl.BlockSpec((nctx, NA_W), lambda i: (0, 0)),
        out_shape=jax.ShapeDtypeStruct((nctx, NA_W), BF16),
        compiler_params=_cparams(("arbitrary",)),
        name="na_ctx",
    )(naqcT, nakc, navcT)


def _na_bias_tables(rel_bias, rows):
    ndr = 2 * NA_WIN_H - 1
    ndc = 2 * NA_WIN_W - 1
    col = np.arange(GRID_W)
    start_c = np.clip(col - NA_WIN_W // 2, 0, GRID_W - NA_WIN_W)
    in_col = (col[:, None] >= start_c[None, :]) & (col[:, None] < start_c[None, :] + NA_WIN_W)
    dc = col[:, None] - col[None, :] + (NA_WIN_W - 1)
    sel_c = (dc[:, :, None] == np.arange(ndc)).astype(np.float32)
    sel_r, ok = [], []
    for r0, base in ((0, 0), (NA_QROWS, 0), (rows - NA_QROWS, rows - NA_KROWS)):
        qr = r0 + np.arange(NA_QROWS)
        kr = base + np.arange(NA_KROWS)
        start_r = np.clip(qr - NA_WIN_H // 2, 0, rows - NA_WIN_H)
        in_row = (kr[:, None] >= start_r[None, :]) & (kr[:, None] < start_r[None, :] + NA_WIN_H)
        dr = kr[:, None] - qr[None, :] + (NA_WIN_H - 1)
        sel_r.append((dr[:, :, None] == np.arange(ndr)).astype(np.float32))
        ok.append(in_row[:, None, :, None] & in_col[None, :, None, :])
    sel_r = np.stack(sel_r)
    ok = np.stack(ok).reshape(3, 1, NA_TK, NA_TQ)
    hp = lax.Precision.HIGHEST
    by_col = jnp.einsum('hrc,kqc->hrkq', rel_bias.astype(F32), sel_c, precision=hp)
    tab = jnp.einsum('hrkq,vabr->vhakbq', by_col, sel_r, precision=hp)
    tab = tab.reshape(3, NA_HEADS, NA_TK, NA_TQ) * LOG2E
    return jnp.where(ok, tab, NEG)


def _mla_kernel(*refs, tk, has_ctx):
    if has_ctx:
        qT_ref, k_ref, vT_ref, kc_ref, vcT_ref, o_ref = refs[:6]
    else:
        qT_ref, k_ref, vT_ref, o_ref = refs[:4]
    m_ref, l_ref, acc_ref, al_ref, s_a, s_b, p_a, p_b = refs[-8:]
    tq = qT_ref.shape[1]
    nk = k_ref.shape[0] // tk

    if has_ctx:
        s = jnp.dot(kc_ref[...], qT_ref[...], preferred_element_type=F32)
        m = s.max(axis=0, keepdims=True)
        p = jnp.exp2(s - m)
        m_ref[...] = m
        l_ref[...] = p.sum(axis=0, keepdims=True)
        acc_ref[...] = jnp.dot(vcT_ref[...], p.astype(BF16), preferred_element_type=F32)
    else:
        m_ref[...] = jnp.full((1, tq), NEG, F32)
        l_ref[...] = jnp.zeros((1, tq), F32)
        acc_ref[...] = jnp.zeros((MLA_V, tq), F32)

    def scores(chunk, s_ref):
        k0 = pl.multiple_of(chunk * tk, tk)
        s_ref[...] = jnp.dot(k_ref[pl.ds(k0, tk), :], qT_ref[...], preferred_element_type=F32)

    def softmax(s_ref, p_ref):
        s = s_ref[...]
        m_old = m_ref[...]
        m_new = jnp.maximum(m_old, s.max(axis=0, keepdims=True))
        alpha = jnp.exp2(m_old - m_new)
        p = jnp.exp2(s - m_new)
        l_ref[...] = alpha * l_ref[...] + p.sum(axis=0, keepdims=True)
        m_ref[...] = m_new
        p_ref[...] = p.astype(BF16)
        return alpha

    def values(chunk, p_ref, alpha):
        k0 = pl.multiple_of(chunk * tk, tk)
        acc_ref[...] = alpha * acc_ref[...] + jnp.dot(vT_ref[:, pl.ds(k0, tk)], p_ref[...],
                                                      preferred_element_type=F32)

    if nk % 2 == 0:
        scores(0, s_a)
        p_b[...] = jnp.zeros(p_b.shape, BF16)
        al_ref[...] = jnp.ones((1, tq), F32)

        def body(it, carry):
            c0 = 2 * it
            scores(c0 + 1, s_b)
            a0 = softmax(s_a, p_a)
            values(jnp.maximum(c0 - 1, 0), p_b, al_ref[...])
            scores(jnp.minimum(c0 + 2, nk - 1), s_a)
            a1 = softmax(s_b, p_b)
            values(c0, p_a, a0)
            al_ref[...] = a1
            return carry

        lax.fori_loop(0, nk // 2, body, 0)
        values(nk - 1, p_b, al_ref[...])
    else:
        for c in range(nk):
            scores(c, s_a)
            values(c, p_a, softmax(s_a, p_a))
    o_ref[...] = (acc_ref[...] / l_ref[...]).T.astype(BF16)


def _mla_attention(mqT, mk, mvT, ctx_kv, tq, tk):
    nq = mqT.shape[1]
    nkeys = mk.shape[0]
    has_ctx = ctx_kv is not None
    in_specs = [
        pl.BlockSpec((MLA_QPAD, tq), lambda h, i: (h, i)),
        pl.BlockSpec((nkeys, MLA_QPAD), lambda h, i: (0, h)),
        pl.BlockSpec((MLA_V, nkeys), lambda h, i: (h, 0)),
    ]
    args = [mqT, mk, mvT]
    if has_ctx:
        kc, vcT = ctx_kv
        nctx = kc.shape[0]
        in_specs += [
            pl.BlockSpec((nctx, MLA_QPAD), lambda h, i: (0, h)),
            pl.BlockSpec((MLA_V, nctx), lambda h, i: (h, 0)),
        ]
        args += [kc, vcT]
    return pl.pallas_call(
        functools.partial(_mla_kernel, tk=tk, has_ctx=has_ctx),
        grid=(MLA_HEADS, nq // tq),
        in_specs=in_specs,
        out_specs=pl.BlockSpec((tq, MLA_V), lambda h, i: (i, h)),
        out_shape=jax.ShapeDtypeStruct((nq, MLA_HEADS * MLA_V), BF16),
        scratch_shapes=[pltpu.VMEM((1, tq), F32), pltpu.VMEM((1, tq), F32), pltpu.VMEM((MLA_V, tq), F32),
                        pltpu.VMEM((1, tq), F32),
                        pltpu.VMEM((tk, tq), F32), pltpu.VMEM((tk, tq), F32),
                        pltpu.VMEM((tk, tq), BF16), pltpu.VMEM((tk, tq), BF16)],
        compiler_params=_cparams(("parallel", "arbitrary")),
        name="mla_attention",
    )(*args)


def _conv3_rows(x, prev_row, next_row, w_ref):
    tm = x.shape[0]
    rows = lax.broadcasted_iota(jnp.int32, (tm, 1), 0)
    xm1 = jnp.where(rows == 0, prev_row, pltpu.roll(x, 1, 0))
    xp1 = jnp.where(rows == tm - 1, next_row, pltpu.roll(x, tm - 1, 0))
    return w_ref[0:1, :] * xm1 + w_ref[1:2, :] * x + w_ref[2:3, :] * xp1


HALO = 16


def _merge_kernel(ya_ref, yb_ref, ubc_ref, prev_ref, next_ref, gate_ref, x_ref, g1_ref, cw_ref,
                  wa_ref, wb_ref, wc_ref, wo_ref, o_ref, *, row):
    i = pl.program_id(0)
    nblk = pl.num_programs(0)
    ubc = ubc_ref[...].astype(F32)
    u = ubc[:, :CONV_CH]
    gb = ubc[:, CONV_CH:2 * CONV_CH]
    gc = ubc[:, 2 * CONV_CH:]
    cu = gc * u
    pr = prev_ref[...].astype(F32)[HALO - 1:HALO, :]
    nx = next_ref[...].astype(F32)[0:1, :]
    cu_prev = pr[:, 2 * CONV_CH:] * pr[:, :CONV_CH] * (i > 0).astype(F32)
    cu_next = nx[:, 2 * CONV_CH:] * nx[:, :CONV_CH] * (i < nblk - 1).astype(F32)
    yc = (gb * _conv3_rows(cu, cu_prev, cu_next, cw_ref)).astype(BF16)

    a = jnp.dot(ya_ref[...], wa_ref[...], preferred_element_type=F32)
    b = jnp.dot(yb_ref[...], wb_ref[...], preferred_element_type=F32)
    c = jnp.dot(yc, wc_ref[...], preferred_element_type=F32)
    d = D_MODEL
    merged = (jax.nn.sigmoid(gate_ref[:, :d].astype(F32)) * a
              + jax.nn.sigmoid(gate_ref[:, d:2 * d].astype(F32)) * b
              + jax.nn.sigmoid(gate_ref[:, 2 * d:].astype(F32)) * c)
    mix = jnp.dot(merged.astype(BF16), wo_ref[...], preferred_element_type=F32)
    o_ref[...] = x_ref[...] + g1_ref[row:row + 1, :] * mix


def _merge(ya, yb, p, gates, x2d, mod, g1_blk, row, conv_w8, wa, wb, wc, wo, tm):
    n, d = x2d.shape
    ubc_blk = P_UBC // (3 * CONV_CH)
    hb = tm // HALO
    last = n // HALO - 1
    return pl.pallas_call(
        functools.partial(_merge_kernel, row=row),
        grid=(n // tm,),
        in_specs=[
            pl.BlockSpec((tm, NA_W), lambda i: (i, 0)),
            pl.BlockSpec((tm, MLA_HEADS * MLA_V), lambda i: (i, 0)),
            pl.BlockSpec((tm, 3 * CONV_CH), lambda i: (i, ubc_blk)),
            pl.BlockSpec((HALO, 3 * CONV_CH), lambda i: (jnp.maximum(i * hb - 1, 0), ubc_blk)),
            pl.BlockSpec((HALO, 3 * CONV_CH), lambda i: (jnp.minimum((i + 1) * hb, last), ubc_blk)),
            pl.BlockSpec((tm, G_COLS), lambda i: (i, 0)),
            pl.BlockSpec((tm, d), lambda i: (i, 0)),
            pl.BlockSpec((8, d), lambda i: (0, g1_blk), pipeline_mode=pl.Buffered(1)),
            _const_spec((8, CONV_CH)),
            _const_spec(wa.shape),
            _const_spec(wb.shape),
            _const_spec(wc.shape),
            _const_spec(wo.shape),
        ],
        out_specs=pl.BlockSpec((tm, d), lambda i: (i, 0)),
        out_shape=jax.ShapeDtypeStruct((n, d), F32),
        compiler_params=_cparams(("parallel",)),
        name="merge_out",
    )(ya, yb, p, p, p, gates, x2d, mod, conv_w8, wa, wb, wc, wo)


FFN_KC = D_FF // 4


def _ffn_down_kernel(a_ref, prev_ref, next_ref, val_ref, cw_ref, wd_ref, x_ref, g2_ref, o_ref, acc_ref, *, row):
    i = pl.program_id(0)
    kk = pl.program_id(1)
    nblk = pl.num_programs(0)
    a = a_ref[...].astype(F32)
    pr = prev_ref[...].astype(F32)[HALO - 1:HALO, :] * (i > 0).astype(F32)
    nx = next_ref[...].astype(F32)[0:1, :] * (i < nblk - 1).astype(F32)
    a = _conv3_rows(a, pr, nx, cw_ref) + cw_ref[3:4, :]
    h = (a * jax.nn.sigmoid(a) * val_ref[...].astype(F32)).astype(BF16)
    part = jnp.dot(h, wd_ref[...], preferred_element_type=F32)

    @pl.when(kk == 0)
    def _():
        acc_ref[...] = part

    @pl.when(kk > 0)
    def _():
        acc_ref[...] += part

    @pl.when(kk == pl.num_programs(1) - 1)
    def _():
        o_ref[...] = x_ref[...] + g2_ref[row:row + 1, :] * acc_ref[...]


def _ffn_down(av, x2d, mod, g2_blk, row, cw8, wd, tm):
    n, d = x2d.shape
    nkc = D_FF // FFN_KC
    hb = tm // HALO
    last = n // HALO - 1
    return pl.pallas_call(
        functools.partial(_ffn_down_kernel, row=row),
        grid=(n // tm, nkc),
        in_specs=[
            pl.BlockSpec((tm, FFN_KC), lambda i, k: (i, k)),
            pl.BlockSpec((HALO, FFN_KC), lambda i, k: (jnp.maximum(i * hb - 1, 0), k)),
            pl.BlockSpec((HALO, FFN_KC), lambda i, k: (jnp.minimum((i + 1) * hb, last), k)),
            pl.BlockSpec((tm, FFN_KC), lambda i, k: (i, nkc + k)),
            pl.BlockSpec((8, FFN_KC), lambda i, k: (0, k)),
            pl.BlockSpec((FFN_KC, d), lambda i, k: (k, 0)),
            pl.BlockSpec((tm, d), lambda i, k: (i, 0)),
            pl.BlockSpec((8, d), lambda i, k: (0, g2_blk), pipeline_mode=pl.Buffered(1)),
        ],
        out_specs=pl.BlockSpec((tm, d), lambda i, k: (i, 0)),
        out_shape=jax.ShapeDtypeStruct((n, d), F32),
        scratch_shapes=[pltpu.VMEM((tm, d), F32)],
        compiler_params=_cparams(("parallel", "arbitrary")),
        name="ffn_down",
    )(av, av, av, av, cw8, wd, x2d, mod)


def _layout_w_in(w):
    d = w.shape[0]
    z64 = jnp.zeros((d, 64), w.dtype)
    cols = [w[:, R_NA_K:R_NA_V], w[:, R_NA_V:R_CKV], w[:, R_NA_Q:R_MLA_Q]]
    for h in range(MLA_HEADS):
        cols += [w[:, R_MLA_Q + h * MLA_QK:R_MLA_Q + (h + 1) * MLA_QK], z64]
    cols += [w[:, R_CKV:R_KROPE], w[:, R_KROPE:R_NA_Q], z64, z64, z64,
             w[:, R_CONV_U:R_GATE], w[:, R_GATE:]]
    return jnp.concatenate(cols, axis=1).astype(BF16)


def _pad_rows(a, rows):
    return jnp.concatenate([a, jnp.zeros((rows - a.shape[0],) + a.shape[1:], a.dtype)], axis=0)


def _rope_tables_T(n_tok):
    t = jnp.arange(n_tok)
    row = (t // GRID_W).astype(F32)
    col = (t % GRID_W).astype(F32)
    inv = ROPE_BASE ** (-jnp.arange(ROPE_NFREQ, dtype=F32) / ROPE_NFREQ)
    ang = jnp.stack([inv[:, None] * row[None, :], inv[:, None] * col[None, :]], axis=0)
    return jnp.cos(ang), jnp.sin(ang)


def kernel(x, c, ctx, c_ctx, w_mod, b_mod, norm1_g, w_in, na_q_g, na_k_g, na_rel_bias, mla_kv_g, w_kv_up,
           mla_q_g, mla_k_g, conv_w, w_br_a, w_br_b, w_br_c, w_o, norm2_g, w_up, ffn_conv_w, ffn_conv_b, w_down):
    bsz, seq, d = x.shape
    nctx = ctx.shape[1]
    depth = w_mod.shape[0]
    rows = seq // GRID_W
    assert bsz == 1 and d == D_MODEL and seq % NA_TQ == 0 and rows >= NA_KROWS and nctx % 256 == 0

    xs = x[0]
    cs = ctx[0]
    tm_x = min(1024, seq)
    tm_half = min(512, seq)
    tm_merge = min(256, seq)
    tq_mla = min(512, seq)
    tk_mla = min(512, seq)

    cos_x, sin_x = _rope_tables_T(seq)
    cos_c = jnp.ones((2, ROPE_NFREQ, nctx), F32)
    sin_c = jnp.zeros((2, ROPE_NFREQ, nctx), F32)
    c_cols = jnp.zeros((d, 128), F32).at[:, 0].set(c[0]).at[:, 1].set(c_ctx)

    for l in range(depth):
        last = l == depth - 1
        mod = _modulation(c_cols, w_mod[l], b_mod[l])

        w_in_l = _layout_w_in(w_in[l])
        kv_up = w_kv_up[l].reshape(MLA_KV_RANK, MLA_HEADS, MLA_NOPE + MLA_V)
        wknT = kv_up[:, :, :MLA_NOPE].reshape(MLA_KV_RANK, -1).T.astype(BF16)
        wvT = kv_up[:, :, MLA_NOPE:].reshape(MLA_KV_RANK, -1).T.astype(BF16)
        gains = (na_q_g[l].reshape(-1, 1), na_k_g[l].reshape(-1, 1), mla_kv_g[l].reshape(-1, 1),
                 _pad_rows(mla_q_g[l].reshape(-1, 1), MLA_QPAD), _pad_rows(mla_k_g[l].reshape(-1, 1), MLA_QPAD))
        bias = _na_bias_tables(na_rel_bias[l], rows)
        conv_w8 = _pad_rows(conv_w[l], 8)
        ffn_cw8 = _pad_rows(jnp.concatenate([ffn_conv_w[l], ffn_conv_b[l][None]], axis=0), 8)
        wa, wb, wc, wo = (w.astype(BF16) for w in (w_br_a[l], w_br_b[l], w_br_c[l], w_o[l]))
        w_up_l = w_up[l].astype(BF16)
        w_down_l = w_down[l].astype(BF16)

        pc, gc = _in_proj(cs, norm1_g[l], mod, 0, 1, w_in_l, nctx)
        naqcT, nakc, navcT, mqcT, mkc, mvcT = _prep(pc, cos_c, sin_c, gains, wknT, wvT, nctx)
        px, gx = _in_proj(xs, norm1_g[l], mod, 0, 0, w_in_l, tm_x)
        naqT, nak, navT, mqT, mk, mvT = _prep(px, cos_x, sin_x, gains, wknT, wvT, tm_half)

        ya = _na_latent(naqT, nak, navT, nakc, navcT, bias)
        yb = _mla_attention(mqT, mk, mvT, (mkc, mvcT), tq_mla, tk_mla)
        xs = _merge(ya, yb, px, gx, xs, mod, 2, 0, conv_w8, wa, wb, wc, wo, tm_merge)
        av = _ffn_up(xs, norm2_g[l], mod, 3, 0, w_up_l, tm_x)
        xs = _ffn_down(av, xs, mod, 5, 0, ffn_cw8, w_down_l, tm_half)

        if not last:
            yac = _na_ctx(naqcT, nakc, navcT)
            ybc = _mla_attention(mqcT, mkc, mvcT, None, nctx, nctx)
            cs = _merge(yac, ybc, pc, gc, cs, mod, 2, 1, conv_w8, wa, wb, wc, wo, nctx)
            avc = _ffn_up(cs, norm2_g[l], mod, 3, 1, w_up_l, nctx)
            cs = _ffn_down(avc, cs, mod, 5, 1, ffn_cw8, w_down_l, nctx)

    return xs[None]
```

```python
import functools

import math

import jax
import jax.numpy as jnp
import numpy as np
from jax import lax
from jax.experimental import pallas as pl
from jax.experimental.pallas import tpu as pltpu

F32 = jnp.float32
BF16 = jnp.bfloat16

D_MODEL = 2048
GRID_W = 64
EPS = 1e-6
NA_HEADS = 8
NA_HEAD_DIM = 64
NA_WIN_H = 8
NA_WIN_W = 16
NA_W = NA_HEADS * NA_HEAD_DIM
MLA_HEADS = 4
MLA_NOPE = 128
MLA_ROPE = 64
MLA_V = 128
MLA_KV_RANK = 256
MLA_QK = MLA_NOPE + MLA_ROPE
MLA_QPAD = 256
ROPE_NFREQ = MLA_ROPE // 4
ROPE_BASE = 10000.0
CONV_CH = 512
D_FF = 5632

R_NA_K = 0
R_NA_V = R_NA_K + NA_W
R_CKV = R_NA_V + NA_W
R_KROPE = R_CKV + MLA_KV_RANK
R_NA_Q = R_KROPE + MLA_ROPE
R_MLA_Q = R_NA_Q + NA_W
R_CONV_U = R_MLA_Q + MLA_HEADS * MLA_QK
R_GATE = R_CONV_U + 3 * CONV_CH

P_NA_K = 0
P_NA_V = 512
P_NA_Q = 1024
P_MLA_Q = 1536
P_CKV = 2560
P_KROPE = 2816
P_QKV_END = 3072
P_UBC = 3072
P_COLS = 4608
G_COLS = 3 * D_MODEL

NA_QROWS = 4
NA_KROWS = 12
NA_TQ = NA_QROWS * GRID_W
NA_TK = NA_KROWS * GRID_W
NEG = -1e30
LOG2E = math.log2(math.e)
MLA_BOUND_MARGIN = 1.02
MLA_MAX_SHIFT = 50.0

VMEM_LIMIT = 56 * 1024 * 1024


def _cparams(sem):
    return pltpu.CompilerParams(dimension_semantics=sem, vmem_limit_bytes=VMEM_LIMIT)


def _const_spec(shape):
    nd = len(shape)
    return pl.BlockSpec(shape, lambda *_: (0,) * nd, pipeline_mode=pl.Buffered(1))


MOD_KC = 256


def _mod_kernel(c_ref, w_ref, b_ref, o_ref):
    tn = o_ref.shape[1]
    nk = w_ref.shape[0] // MOD_KC

    def body(kc, acc):
        k0 = pl.multiple_of(kc * MOD_KC, MOD_KC)
        w = w_ref[pl.ds(k0, MOD_KC), :]
        c = c_ref[pl.ds(k0, MOD_KC), :]
        s = c * jax.nn.sigmoid(c)
        out = []
        for r in range(2):
            prod = s[:, r:r + 1] * w
            out.append(acc[r] + jnp.sum(prod.reshape(MOD_KC // 8, 8, tn), axis=0))
        return tuple(out)

    z = jnp.zeros((8, tn), F32)
    acc = lax.fori_loop(0, nk, body, (z, z))
    r0, r1 = [jnp.sum(a, axis=0, keepdims=True) + b_ref[...] for a in acc]
    ridx = lax.broadcasted_iota(jnp.int32, (8, tn), 0)
    o_ref[...] = jnp.where(ridx == 0, r0, jnp.where(ridx == 1, r1, 0.0))


def _modulation(c_cols, w_mod, b_mod, layer):
    depth, d, n = w_mod.shape
    tn = 1536
    return pl.pallas_call(
        _mod_kernel,
        grid=(n // tn,),
        in_specs=[
            _const_spec((d, 128)),
            pl.BlockSpec((None, d, tn), lambda j: (layer, 0, j)),
            pl.BlockSpec((None, 1, tn), lambda j: (layer, 0, j)),
        ],
        out_specs=pl.BlockSpec((8, tn), lambda j: (0, j)),
        out_shape=jax.ShapeDtypeStruct((8, n), F32),
        compiler_params=_cparams(("arbitrary",)),
        name="modulation",
    )(c_cols, w_mod, b_mod.reshape(depth, 1, n))


NORM_RC = 128


def _norm_rows(x_ref, g_ref, sh_ref, sc_ref, hx_ref, row):
    tm = x_ref.shape[0]
    g = g_ref[...]
    sc1 = 1.0 + sc_ref[row:row + 1, :]
    sh = sh_ref[row:row + 1, :]

    def body(r, carry):
        r0 = pl.multiple_of(r * NORM_RC, NORM_RC)
        x = x_ref[pl.ds(r0, NORM_RC), :]
        ms = jnp.mean(x * x, axis=-1, keepdims=True)
        y = (x * lax.rsqrt(ms + EPS)) * g
        hx_ref[pl.ds(r0, NORM_RC), :] = (y * sc1 + sh).astype(BF16)
        return carry

    lax.fori_loop(0, tm // NORM_RC, body, 0)


def _in_proj_kernel(x_ref, g_ref, sh_ref, sc_ref, w_ref, p_ref, gate_ref, hx_ref, *, row, n_p):
    j = pl.program_id(1)

    @pl.when(j == 0)
    def _():
        _norm_rows(x_ref, g_ref, sh_ref, sc_ref, hx_ref, row)

    acc = jnp.dot(hx_ref[...], w_ref[...], preferred_element_type=F32)

    @pl.when(j < n_p)
    def _():
        p_ref[...] = acc.astype(BF16)

    @pl.when(j >= n_p)
    def _():
        gate_ref[...] = acc.astype(BF16)


def _in_proj(x2d, norm_g, mod, sh_blk, row, w_bf16, tm):
    n, d = x2d.shape
    tn = 1536
    n_p = P_COLS // tn
    n_g = G_COLS // tn
    return pl.pallas_call(
        functools.partial(_in_proj_kernel, row=row, n_p=n_p),
        grid=(n // tm, n_p + n_g),
        in_specs=[
            pl.BlockSpec((tm, d), lambda i, j: (i, 0)),
            _const_spec((1, d)),
            pl.BlockSpec((8, d), lambda i, j: (0, sh_blk), pipeline_mode=pl.Buffered(1)),
            pl.BlockSpec((8, d), lambda i, j: (0, sh_blk + 1), pipeline_mode=pl.Buffered(1)),
            pl.BlockSpec((d, tn), lambda i, j: (0, j)),
        ],
        out_specs=[
            pl.BlockSpec((tm, tn), lambda i, j: (i, jnp.minimum(j, n_p - 1))),
            pl.BlockSpec((tm, tn), lambda i, j: (i, jnp.maximum(j - n_p, 0))),
        ],
        out_shape=[
            jax.ShapeDtypeStruct((n, P_COLS), BF16),
            jax.ShapeDtypeStruct((n, G_COLS), BF16),
        ],
        scratch_shapes=[pltpu.VMEM((tm, d), BF16)],
        compiler_params=_cparams(("parallel", "arbitrary")),
        name="in_proj",
    )(x2d, norm_g.reshape(1, d), mod, mod, w_bf16)


def _ffn_up_kernel(x_ref, g_ref, sh_ref, sc_ref, w_ref, o_ref, hx_ref, *, row):
    @pl.when(pl.program_id(1) == 0)
    def _():
        _norm_rows(x_ref, g_ref, sh_ref, sc_ref, hx_ref, row)

    o_ref[...] = jnp.dot(hx_ref[...], w_ref[...], preferred_element_type=F32).astype(BF16)


def _ffn_up(x2d, norm_g, mod, sh_blk, row, w_bf16, tm):
    n, d = x2d.shape
    ncol = w_bf16.shape[1]
    tn = 1024
    return pl.pallas_call(
        functools.partial(_ffn_up_kernel, row=row),
        grid=(n // tm, ncol // tn),
        in_specs=[
            pl.BlockSpec((tm, d), lambda i, j: (i, 0)),
            _const_spec((1, d)),
            pl.BlockSpec((8, d), lambda i, j: (0, sh_blk), pipeline_mode=pl.Buffered(1)),
            pl.BlockSpec((8, d), lambda i, j: (0, sh_blk + 1), pipeline_mode=pl.Buffered(1)),
            pl.BlockSpec((d, tn), lambda i, j: (0, j)),
        ],
        out_specs=pl.BlockSpec((tm, tn), lambda i, j: (i, j)),
        out_shape=jax.ShapeDtypeStruct((n, ncol), BF16),
        scratch_shapes=[pltpu.VMEM((tm, d), BF16)],
        compiler_params=_cparams(("parallel", "arbitrary")),
        name="ffn_up",
    )(x2d, norm_g.reshape(1, d), mod, mod, w_bf16)


def _rope_rows(r, cos_ref, sin_ref):
    out = []
    for part in range(2):
        cs = cos_ref[part]
        sn = sin_ref[part]
        x1 = r[32 * part:32 * part + 16]
        x2 = r[32 * part + 16:32 * part + 32]
        out.append(x1 * cs - x2 * sn)
        out.append(x2 * cs + x1 * sn)
    return out


def _prep_kernel(p_ref, cos_ref, sin_ref, naqg_ref, nakg_ref, kvg_ref, mqg_ref, mkg_ref,
                 wkn_ref, wv_ref,
                 naqT_ref, nak_ref, navT_ref, mqT_ref, mk_ref, mvT_ref, qn_ref, kn_ref):
    tm = p_ref.shape[0]

    def head_norm_T(cols, g_ref, scale):
        xt = p_ref[:, cols:cols + NA_W].astype(F32).T
        x3 = xt.reshape(NA_HEADS, NA_HEAD_DIM, tm)
        ms = jnp.mean(x3 * x3, axis=1, keepdims=True)
        y = x3 * lax.rsqrt(ms + EPS) * g_ref[...][None]
        if scale != 1.0:
            y = y * scale
        return y.reshape(NA_W, tm)

    naqT_ref[...] = head_norm_T(P_NA_Q, naqg_ref, NA_HEAD_DIM ** -0.5 * LOG2E).astype(BF16)
    nak_ref[...] = head_norm_T(P_NA_K, nakg_ref, 1.0).T.astype(BF16)
    navT_ref[...] = p_ref[:, P_NA_V:P_NA_V + NA_W].astype(F32).T.astype(BF16)

    mqg = mqg_ref[...]
    qscale = MLA_QK ** -0.5 * LOG2E
    for h in range(MLA_HEADS):
        c0 = P_MLA_Q + h * MLA_QPAD
        qt = p_ref[:, c0:c0 + MLA_QPAD].astype(F32).T
        ms = jnp.sum(qt * qt, axis=0, keepdims=True) * (1.0 / MLA_QK)
        y = qt * lax.rsqrt(ms + EPS) * mqg
        r0 = h * MLA_QPAD
        mqT_ref[r0:r0 + MLA_NOPE, :] = (y[:MLA_NOPE] * qscale).astype(BF16)
        parts = _rope_rows(y[MLA_NOPE:MLA_QK], cos_ref, sin_ref)
        for k, part in enumerate(parts):
            a = r0 + MLA_NOPE + 16 * k
            mqT_ref[a:a + 16, :] = (part * qscale).astype(BF16)
        mqT_ref[r0 + MLA_QK:r0 + MLA_QPAD, :] = jnp.zeros((MLA_QPAD - MLA_QK, tm), BF16)
        qnorm = jnp.sqrt(jnp.sum(y * y, axis=0, keepdims=True)) * qscale
        qn_ref[8 * h:8 * h + 8, :] = jnp.broadcast_to(qnorm, (8, tm))

    ct = p_ref[:, P_CKV:P_CKV + MLA_KV_RANK].astype(F32).T
    ms = jnp.mean(ct * ct, axis=0, keepdims=True)
    cn = (ct * lax.rsqrt(ms + EPS) * kvg_ref[...]).astype(BF16)
    mvT_ref[...] = jnp.dot(wv_ref[...], cn, preferred_element_type=F32).astype(BF16)
    knT = jnp.dot(wkn_ref[...], cn, preferred_element_type=F32)
    krT = p_ref[:, P_KROPE:P_KROPE + 128].astype(F32).T[:MLA_ROPE]
    kr_ss = jnp.sum(krT * krT, axis=0, keepdims=True)
    mkg = mkg_ref[...]
    pad = (lax.broadcasted_iota(jnp.int32, (MLA_QPAD - MLA_QK, tm), 0) == 0).astype(F32)
    for h in range(MLA_HEADS):
        kn = knT[h * MLA_NOPE:(h + 1) * MLA_NOPE]
        ms = (jnp.sum(kn * kn, axis=0, keepdims=True) + kr_ss) * (1.0 / MLA_QK)
        rinv = lax.rsqrt(ms + EPS)
        kn = kn * rinv * mkg[:MLA_NOPE]
        kr = krT * rinv * mkg[MLA_NOPE:MLA_QK]
        knorm = jnp.sqrt(jnp.sum(kn * kn, axis=0, keepdims=True) + jnp.sum(kr * kr, axis=0, keepdims=True))
        kn_ref[8 * h:8 * h + 8, :] = jnp.broadcast_to(knorm, (8, tm))
        pieces = [kn] + _rope_rows(kr, cos_ref, sin_ref) + [pad]
        kT = jnp.concatenate(pieces, axis=0)
        mk_ref[:, h * MLA_QPAD:(h + 1) * MLA_QPAD] = kT.T.astype(BF16)


def _prep(p, cosT, sinT, gains, wknT, wvT, tm):
    n = p.shape[0]
    naqg, nakg, kvg, mqg, mkg = gains
    return pl.pallas_call(
        _prep_kernel,
        grid=(n // tm,),
        in_specs=[
            pl.BlockSpec((tm, P_QKV_END), lambda i: (i, 0)),
            pl.BlockSpec((2, ROPE_NFREQ, tm), lambda i: (0, 0, i)),
            pl.BlockSpec((2, ROPE_NFREQ, tm), lambda i: (0, 0, i)),
            _const_spec((NA_HEAD_DIM, 1)),
            _const_spec((NA_HEAD_DIM, 1)),
            _const_spec((MLA_KV_RANK, 1)),
            _const_spec((MLA_QPAD, 1)),
            _const_spec((MLA_QPAD, 1)),
            _const_spec((MLA_HEADS * MLA_NOPE, MLA_KV_RANK)),
            _const_spec((MLA_HEADS * MLA_V, MLA_KV_RANK)),
        ],
        out_specs=[
            pl.BlockSpec((NA_W, tm), lambda i: (0, i)),
            pl.BlockSpec((tm, NA_W), lambda i: (i, 0)),
            pl.BlockSpec((NA_W, tm), lambda i: (0, i)),
            pl.BlockSpec((MLA_HEADS * MLA_QPAD, tm), lambda i: (0, i)),
            pl.BlockSpec((tm, MLA_HEADS * MLA_QPAD), lambda i: (i, 0)),
            pl.BlockSpec((MLA_HEADS * MLA_V, tm), lambda i: (0, i)),
            pl.BlockSpec((MLA_HEADS * 8, tm), lambda i: (0, i)),
            pl.BlockSpec((MLA_HEADS * 8, tm), lambda i: (0, i)),
        ],
        out_shape=[
            jax.ShapeDtypeStruct((NA_W, n), BF16),
            jax.ShapeDtypeStruct((n, NA_W), BF16),
            jax.ShapeDtypeStruct((NA_W, n), BF16),
            jax.ShapeDtypeStruct((MLA_HEADS * MLA_QPAD, n), BF16),
            jax.ShapeDtypeStruct((n, MLA_HEADS * MLA_QPAD), BF16),
            jax.ShapeDtypeStruct((MLA_HEADS * MLA_V, n), BF16),
            jax.ShapeDtypeStruct((MLA_HEADS * 8, n), F32),
            jax.ShapeDtypeStruct((MLA_HEADS * 8, n), F32),
        ],
        compiler_params=_cparams(("parallel",)),
        name="qkv_prep",
    )(p, cosT, sinT, naqg, nakg, kvg, mqg, mkg, wknT, wvT)


def _na_heads(q_ref, k_refs, vT_refs, bias_ref, o_ref):
    tq = q_ref.shape[1]
    upper = lax.broadcasted_iota(jnp.int32, (128, tq), 0) >= NA_HEAD_DIM
    outs = []
    for h in range(NA_HEADS):
        j, e = divmod(h, 2)
        q2 = q_ref[128 * j:128 * j + 128, :]
        qm = jnp.where(upper == bool(e), q2, jnp.zeros_like(q2))
        s_list = []
        off = 0
        for b, k_ref in enumerate(k_refs):
            s = jnp.dot(k_ref[:, 128 * j:128 * j + 128], qm, preferred_element_type=F32)
            nk = k_ref.shape[0]
            if bias_ref is not None and b < len(k_refs) - 1:
                s = s + bias_ref[0, h, off:off + nk, :]
            off += nk
            s_list.append(s)
        m = s_list[0].max(axis=0, keepdims=True)
        for s in s_list[1:]:
            m = jnp.maximum(m, s.max(axis=0, keepdims=True))
        l = jnp.zeros((1, tq), F32)
        acc = jnp.zeros((NA_HEAD_DIM, tq), F32)
        for s, vT_ref in zip(s_list, vT_refs):
            p = jnp.exp2(s - m)
            l = l + p.sum(axis=0, keepdims=True)
            acc = acc + jnp.dot(vT_ref[NA_HEAD_DIM * h:NA_HEAD_DIM * (h + 1), :], p.astype(BF16),
                                preferred_element_type=F32)
        outs.append(acc / l)
    o_ref[...] = jnp.concatenate(outs, axis=0).T.astype(BF16)


def _na_latent_kernel(q_ref, k0, k1, k2, kc, v0, v1, v2, vc, bias_ref, o_ref):
    _na_heads(q_ref, [k0, k1, k2, kc], [v0, v1, v2, vc], bias_ref, o_ref)


def _na_ctx_kernel(q_ref, kc, vc, o_ref):
    _na_heads(q_ref, [kc], [vc], None, o_ref)


def _na_latent(naqT, nak, navT, nakc, navcT, bias):
    n = naqT.shape[1]
    nb = n // NA_TQ
    nctx = nakc.shape[0]

    def kb(o):
        return lambda i: (jnp.clip(i - 1, 0, nb - 3) + o, 0)

    def vb(o):
        return lambda i: (0, jnp.clip(i - 1, 0, nb - 3) + o)

    def variant(i):
        return (jnp.where(i == 0, 0, jnp.where(i == nb - 1, 2, 1)), 0, 0, 0)

    return pl.pallas_call(
        _na_latent_kernel,
        grid=(nb,),
        in_specs=[
            pl.BlockSpec((NA_W, NA_TQ), lambda i: (0, i)),
            pl.BlockSpec((NA_TQ, NA_W), kb(0)),
            pl.BlockSpec((NA_TQ, NA_W), kb(1)),
            pl.BlockSpec((NA_TQ, NA_W), kb(2)),
            _const_spec((nctx, NA_W)),
            pl.BlockSpec((NA_W, NA_TQ), vb(0)),
            pl.BlockSpec((NA_W, NA_TQ), vb(1)),
            pl.BlockSpec((NA_W, NA_TQ), vb(2)),
            _const_spec((NA_W, nctx)),
            pl.BlockSpec((1, NA_HEADS, NA_TK, NA_TQ), variant),
        ],
        out_specs=pl.BlockSpec((NA_TQ, NA_W), lambda i: (i, 0)),
        out_shape=jax.ShapeDtypeStruct((n, NA_W), BF16),
        compiler_params=_cparams(("parallel",)),
        name="na_latent",
    )(naqT, nak, nak, nak, nakc, navT, navT, navT, navcT, bias)


def _na_ctx(naqcT, nakc, navcT):
    nctx = nakc.shape[0]
    return pl.pallas_call(
        _na_ctx_kernel,
        grid=(1,),
        in_specs=[_const_spec((NA_W, nctx)), _const_spec((nctx, NA_W)), _const_spec((NA_W, nctx))],
        out_specs=pl.BlockSpec((nctx, NA_W), lambda i: (0, 0)),
        out_shape=jax.ShapeDtypeStruct((nctx, NA_W), BF16),
        compiler_params=_cparams(("arbitrary",)),
        name="na_ctx",
    )(naqcT, nakc, navcT)


def _na_bias_tables(rel_bias, rows):
    ndr = 2 * NA_WIN_H - 1
    ndc = 2 * NA_WIN_W - 1
    col = np.arange(GRID_W)
    start_c = np.clip(col - NA_WIN_W // 2, 0, GRID_W - NA_WIN_W)
    in_col = (col[:, None] >= start_c[None, :]) & (col[:, None] < start_c[None, :] + NA_WIN_W)
    dc = col[:, None] - col[None, :] + (NA_WIN_W - 1)
    sel_c = (dc[:, :, None] == np.arange(ndc)).astype(np.float32)
    sel_r, ok = [], []
    for r0, base in ((0, 0), (NA_QROWS, 0), (rows - NA_QROWS, rows - NA_KROWS)):
        qr = r0 + np.arange(NA_QROWS)
        kr = base + np.arange(NA_KROWS)
        start_r = np.clip(qr - NA_WIN_H // 2, 0, rows - NA_WIN_H)
        in_row = (kr[:, None] >= start_r[None, :]) & (kr[:, None] < start_r[None, :] + NA_WIN_H)
        dr = kr[:, None] - qr[None, :] + (NA_WIN_H - 1)
        sel_r.append((dr[:, :, None] == np.arange(ndr)).astype(np.float32))
        ok.append(in_row[:, None, :, None] & in_col[None, :, None, :])
    sel_r = np.stack(sel_r)
    ok = np.stack(ok).reshape(3, 1, NA_TK, NA_TQ)
    hp = lax.Precision.HIGHEST
    by_col = jnp.einsum('hrc,kqc->hrkq', rel_bias.astype(F32), sel_c, precision=hp)
    tab = jnp.einsum('hrkq,vabr->vhakbq', by_col, sel_r, precision=hp)
    tab = tab.reshape(3, NA_HEADS, NA_TK, NA_TQ) * LOG2E
    return jnp.where(ok, tab, NEG)


def _mla_kernel(*refs, tk, has_ctx):
    if has_ctx:
        qT_ref, qn_ref, k_ref, vT_ref, kn_ref, kc_ref, vcT_ref, knc_ref, o_ref = refs[:9]
    else:
        qT_ref, qn_ref, k_ref, vT_ref, kn_ref, o_ref = refs[:6]
    m_ref, l_ref, acc_ref, al_ref, s_a, s_b, p_a, p_b, qs_ref = refs[-9:]
    tq = qT_ref.shape[1]
    nk = k_ref.shape[0] // tk

    kmax = jnp.max(kn_ref[0:1, :])
    if has_ctx:
        kmax = jnp.maximum(kmax, jnp.max(knc_ref[0:1, :]))
    bound = qn_ref[0:1, :] * (kmax * MLA_BOUND_MARGIN)
    shift_ok = jnp.max(bound) <= MLA_MAX_SHIFT

    @pl.when(shift_ok)
    def _():
        qs_ref[...] = qT_ref[...]
        row0 = lax.broadcasted_iota(jnp.int32, (16, tq), 0) == 0
        qs_ref[MLA_QK:MLA_QK + 16, :] = jnp.where(row0, -bound, 0.0).astype(BF16)

        def chunk(k_blk, vT_blk):
            p = jnp.exp2(jnp.dot(k_blk, qs_ref[...], preferred_element_type=F32))
            return p.sum(axis=0, keepdims=True), jnp.dot(vT_blk, p.astype(BF16), preferred_element_type=F32)

        if has_ctx:
            l0, a0 = chunk(kc_ref[...], vcT_ref[...])
            l_ref[...] = l0
            acc_ref[...] = a0
        else:
            l_ref[...] = jnp.zeros((1, tq), F32)
            acc_ref[...] = jnp.zeros((MLA_V, tq), F32)
        unroll = next(u for u in (8, 4, 2, 1) if nk % u == 0)

        def body(it, carry):
            l_sum = l_ref[...]
            a_sum = acc_ref[...]
            for u in range(unroll):
                k0 = pl.multiple_of((it * unroll + u) * tk, tk)
                lp, ap = chunk(k_ref[pl.ds(k0, tk), :], vT_ref[:, pl.ds(k0, tk)])
                l_sum = l_sum + lp
                a_sum = a_sum + ap
            l_ref[...] = l_sum
            acc_ref[...] = a_sum
            return carry

        lax.fori_loop(0, nk // unroll, body, 0)

    @pl.when(jnp.logical_not(shift_ok))
    def _():
        if has_ctx:
            s = jnp.dot(kc_ref[...], qT_ref[...], preferred_element_type=F32)
            m = s.max(axis=0, keepdims=True)
            p = jnp.exp2(s - m)
            m_ref[...] = m
            l_ref[...] = p.sum(axis=0, keepdims=True)
            acc_ref[...] = jnp.dot(vcT_ref[...], p.astype(BF16), preferred_element_type=F32)
        else:
            m_ref[...] = jnp.full((1, tq), NEG, F32)
            l_ref[...] = jnp.zeros((1, tq), F32)
            acc_ref[...] = jnp.zeros((MLA_V, tq), F32)

        def scores(chunk, s_ref):
            k0 = pl.multiple_of(chunk * tk, tk)
            s_ref[...] = jnp.dot(k_ref[pl.ds(k0, tk), :], qT_ref[...], preferred_element_type=F32)

        def softmax(s_ref, p_ref):
            s = s_ref[...]
            m_old = m_ref[...]
            m_new = jnp.maximum(m_old, s.max(axis=0, keepdims=True))
            alpha = jnp.exp2(m_old - m_new)
            p = jnp.exp2(s - m_new)
            l_ref[...] = alpha * l_ref[...] + p.sum(axis=0, keepdims=True)
            m_ref[...] = m_new
            p_ref[...] = p.astype(BF16)
            return alpha

        def values(chunk, p_ref, alpha):
            k0 = pl.multiple_of(chunk * tk, tk)
            acc_ref[...] = alpha * acc_ref[...] + jnp.dot(vT_ref[:, pl.ds(k0, tk)], p_ref[...],
                                                          preferred_element_type=F32)

        if nk % 2 == 0:
            scores(0, s_a)
            p_b[...] = jnp.zeros(p_b.shape, BF16)
            al_ref[...] = jnp.ones((1, tq), F32)

            def body(it, carry):
                c0 = 2 * it
                scores(c0 + 1, s_b)
                a0 = softmax(s_a, p_a)
                values(jnp.maximum(c0 - 1, 0), p_b, al_ref[...])
                scores(jnp.minimum(c0 + 2, nk - 1), s_a)
                a1 = softmax(s_b, p_b)
                values(c0, p_a, a0)
                al_ref[...] = a1
                return carry

            lax.fori_loop(0, nk // 2, body, 0)
            values(nk - 1, p_b, al_ref[...])
        else:
            for c in range(nk):
                scores(c, s_a)
                values(c, p_a, softmax(s_a, p_a))

    o_ref[...] = (acc_ref[...] / l_ref[...]).T.astype(BF16)


def _mla_attention(mqT, qn, mk, mvT, kn, ctx_kv, tq, tk):
    nq = mqT.shape[1]
    nkeys = mk.shape[0]
    has_ctx = ctx_kv is not None
    in_specs = [
        pl.BlockSpec((MLA_QPAD, tq), lambda h, i: (h, i)),
        pl.BlockSpec((8, tq), lambda h, i: (h, i)),
        pl.BlockSpec((nkeys, MLA_QPAD), lambda h, i: (0, h)),
        pl.BlockSpec((MLA_V, nkeys), lambda h, i: (h, 0)),
        pl.BlockSpec((8, nkeys), lambda h, i: (h, 0)),
    ]
    args = [mqT, qn, mk, mvT, kn]
    if has_ctx:
        kc, vcT, knc = ctx_kv
        nctx = kc.shape[0]
        in_specs += [
            pl.BlockSpec((nctx, MLA_QPAD), lambda h, i: (0, h)),
            pl.BlockSpec((MLA_V, nctx), lambda h, i: (h, 0)),
            pl.BlockSpec((8, nctx), lambda h, i: (h, 0)),
        ]
        args += [kc, vcT, knc]
    return pl.pallas_call(
        functools.partial(_mla_kernel, tk=tk, has_ctx=has_ctx),
        grid=(MLA_HEADS, nq // tq),
        in_specs=in_specs,
        out_specs=pl.BlockSpec((tq, MLA_V), lambda h, i: (i, h)),
        out_shape=jax.ShapeDtypeStruct((nq, MLA_HEADS * MLA_V), BF16),
        scratch_shapes=[pltpu.VMEM((1, tq), F32), pltpu.VMEM((1, tq), F32), pltpu.VMEM((MLA_V, tq), F32),
                        pltpu.VMEM((1, tq), F32),
                        pltpu.VMEM((tk, tq), F32), pltpu.VMEM((tk, tq), F32),
                        pltpu.VMEM((tk, tq), BF16), pltpu.VMEM((tk, tq), BF16),
                        pltpu.VMEM((MLA_QPAD, tq), BF16)],
        compiler_params=_cparams(("parallel", "arbitrary")),
        name="mla_attention",
    )(*args)


def _conv3_rows(x, prev_row, next_row, w):
    tm = x.shape[0]
    rows = lax.broadcasted_iota(jnp.int32, (tm, 1), 0)
    xm1 = jnp.where(rows == 0, prev_row, pltpu.roll(x, 1, 0))
    xp1 = jnp.where(rows == tm - 1, next_row, pltpu.roll(x, tm - 1, 0))
    return w[0:1, :] * xm1 + w[1:2, :] * x + w[2:3, :] * xp1


HALO = 16


def _merge_kernel(ya_ref, yb_ref, ubc_ref, prev_ref, next_ref, gate_ref, x_ref, g1_ref, cw_ref,
                  wa_ref, wb_ref, wc_ref, wo_ref, o_ref, *, row):
    i = pl.program_id(0)
    nblk = pl.num_programs(0)
    ubc = ubc_ref[...].astype(F32)
    u = ubc[:, :CONV_CH]
    gb = ubc[:, CONV_CH:2 * CONV_CH]
    gc = ubc[:, 2 * CONV_CH:]
    cu = gc * u
    pr = prev_ref[...].astype(F32)[HALO - 1:HALO, :]
    nx = next_ref[...].astype(F32)[0:1, :]
    cu_prev = pr[:, 2 * CONV_CH:] * pr[:, :CONV_CH] * (i > 0).astype(F32)
    cu_next = nx[:, 2 * CONV_CH:] * nx[:, :CONV_CH] * (i < nblk - 1).astype(F32)
    yc = (gb * _conv3_rows(cu, cu_prev, cu_next, cw_ref[...])).astype(BF16)

    a = jnp.dot(ya_ref[...], wa_ref[...], preferred_element_type=F32)
    b = jnp.dot(yb_ref[...], wb_ref[...], preferred_element_type=F32)
    c = jnp.dot(yc, wc_ref[...], preferred_element_type=F32)
    d = D_MODEL
    merged = (jax.nn.sigmoid(gate_ref[:, :d].astype(F32)) * a
              + jax.nn.sigmoid(gate_ref[:, d:2 * d].astype(F32)) * b
              + jax.nn.sigmoid(gate_ref[:, 2 * d:].astype(F32)) * c)
    mix = jnp.dot(merged.astype(BF16), wo_ref[...], preferred_element_type=F32)
    o_ref[...] = x_ref[...] + g1_ref[row:row + 1, :] * mix


def _merge(ya, yb, p, gates, x2d, mod, g1_blk, row, conv_w8, wa, wb, wc, wo, tm):
    n, d = x2d.shape
    ubc_blk = P_UBC // (3 * CONV_CH)
    hb = tm // HALO
    last = n // HALO - 1
    return pl.pallas_call(
        functools.partial(_merge_kernel, row=row),
        grid=(n // tm,),
        in_specs=[
            pl.BlockSpec((tm, NA_W), lambda i: (i, 0)),
            pl.BlockSpec((tm, MLA_HEADS * MLA_V), lambda i: (i, 0)),
            pl.BlockSpec((tm, 3 * CONV_CH), lambda i: (i, ubc_blk)),
            pl.BlockSpec((HALO, 3 * CONV_CH), lambda i: (jnp.maximum(i * hb - 1, 0), ubc_blk)),
            pl.BlockSpec((HALO, 3 * CONV_CH), lambda i: (jnp.minimum((i + 1) * hb, last), ubc_blk)),
            pl.BlockSpec((tm, G_COLS), lambda i: (i, 0)),
            pl.BlockSpec((tm, d), lambda i: (i, 0)),
            pl.BlockSpec((8, d), lambda i: (0, g1_blk), pipeline_mode=pl.Buffered(1)),
            _const_spec((8, CONV_CH)),
            _const_spec(wa.shape),
            _const_spec(wb.shape),
            _const_spec(wc.shape),
            _const_spec(wo.shape),
        ],
        out_specs=pl.BlockSpec((tm, d), lambda i: (i, 0)),
        out_shape=jax.ShapeDtypeStruct((n, d), F32),
        compiler_params=_cparams(("parallel",)),
        name="merge_out",
    )(ya, yb, p, p, p, gates, x2d, mod, conv_w8, wa, wb, wc, wo)


FFN_KC = 512


def _ffn_down_kernel(a_ref, prev_ref, next_ref, val_ref, cw_ref, wd_ref, x_ref, g2_ref, o_ref,
                     acc_ref, h_a, h_b, *, row):
    i = pl.program_id(0)
    nblk = pl.num_programs(0)
    has_prev = (i > 0).astype(F32)
    has_next = (i < nblk - 1).astype(F32)
    nchunks = D_FF // FFN_KC

    def hidden(c, h_ref):
        sl = slice(c * FFN_KC, (c + 1) * FFN_KC)
        w = cw_ref[:, sl]
        a = a_ref[:, sl].astype(F32)
        pr = prev_ref[:, sl].astype(F32)[HALO - 1:HALO, :] * has_prev
        nx = next_ref[:, sl].astype(F32)[0:1, :] * has_next
        a = _conv3_rows(a, pr, nx, w) + w[3:4, :]
        h_ref[...] = (a * jax.nn.sigmoid(a) * val_ref[:, sl].astype(F32)).astype(BF16)

    bufs = (h_a, h_b)
    hidden(0, h_a)
    for c in range(nchunks):
        if c + 1 < nchunks:
            hidden(c + 1, bufs[(c + 1) % 2])
        part = jnp.dot(bufs[c % 2][...], wd_ref[c * FFN_KC:(c + 1) * FFN_KC, :], preferred_element_type=F32)
        if c == 0:
            acc_ref[...] = part
        else:
            acc_ref[...] += part
    o_ref[...] = x_ref[...] + g2_ref[row:row + 1, :] * acc_ref[...]


def _ffn_down(av, x2d, mod, g2_blk, row, cw8, wd, tm):
    n, d = x2d.shape
    hb = tm // HALO
    last = n // HALO - 1
    return pl.pallas_call(
        functools.partial(_ffn_down_kernel, row=row),
        grid=(n // tm,),
        in_specs=[
            pl.BlockSpec((tm, D_FF), lambda i: (i, 0)),
            pl.BlockSpec((HALO, D_FF), lambda i: (jnp.maximum(i * hb - 1, 0), 0)),
            pl.BlockSpec((HALO, D_FF), lambda i: (jnp.minimum((i + 1) * hb, last), 0)),
            pl.BlockSpec((tm, D_FF), lambda i: (i, 1)),
            _const_spec((8, D_FF)),
            _const_spec((D_FF, d)),
            pl.BlockSpec((tm, d), lambda i: (i, 0)),
            pl.BlockSpec((8, d), lambda i: (0, g2_blk), pipeline_mode=pl.Buffered(1)),
        ],
        out_specs=pl.BlockSpec((tm, d), lambda i: (i, 0)),
        out_shape=jax.ShapeDtypeStruct((n, d), F32),
        scratch_shapes=[pltpu.VMEM((tm, d), F32), pltpu.VMEM((tm, FFN_KC), BF16), pltpu.VMEM((tm, FFN_KC), BF16)],
        compiler_params=_cparams(("parallel",)),
        name="ffn_down",
    )(av, av, av, av, cw8, wd, x2d, mod)


def _layout_w_in(w):
    d = w.shape[0]
    z64 = jnp.zeros((d, 64), w.dtype)
    cols = [w[:, R_NA_K:R_NA_V], w[:, R_NA_V:R_CKV], w[:, R_NA_Q:R_MLA_Q]]
    for h in range(MLA_HEADS):
        cols += [w[:, R_MLA_Q + h * MLA_QK:R_MLA_Q + (h + 1) * MLA_QK], z64]
    cols += [w[:, R_CKV:R_KROPE], w[:, R_KROPE:R_NA_Q], z64, z64, z64,
             w[:, R_CONV_U:R_GATE], w[:, R_GATE:]]
    return jnp.concatenate(cols, axis=1).astype(BF16)


def _pad_rows(a, rows):
    return jnp.concatenate([a, jnp.zeros((rows - a.shape[0],) + a.shape[1:], a.dtype)], axis=0)


def _rope_tables_T(n_tok):
    t = jnp.arange(n_tok)
    row = (t // GRID_W).astype(F32)
    col = (t % GRID_W).astype(F32)
    inv = ROPE_BASE ** (-jnp.arange(ROPE_NFREQ, dtype=F32) / ROPE_NFREQ)
    ang = jnp.stack([inv[:, None] * row[None, :], inv[:, None] * col[None, :]], axis=0)
    return jnp.cos(ang), jnp.sin(ang)


def kernel(x, c, ctx, c_ctx, w_mod, b_mod, norm1_g, w_in, na_q_g, na_k_g, na_rel_bias, mla_kv_g, w_kv_up,
           mla_q_g, mla_k_g, conv_w, w_br_a, w_br_b, w_br_c, w_o, norm2_g, w_up, ffn_conv_w, ffn_conv_b, w_down):
    bsz, seq, d = x.shape
    nctx = ctx.shape[1]
    depth = w_mod.shape[0]
    rows = seq // GRID_W
    assert bsz == 1 and d == D_MODEL and seq % NA_TQ == 0 and rows >= NA_KROWS and nctx % 256 == 0

    xs = x[0]
    cs = ctx[0]
    tm_x = min(1024, seq)
    tm_half = min(512, seq)
    tm_merge = min(256, seq)
    tq_mla = min(512, seq)
    tk_mla = min(512, seq)

    cos_x, sin_x = _rope_tables_T(seq)
    cos_c = jnp.ones((2, ROPE_NFREQ, nctx), F32)
    sin_c = jnp.zeros((2, ROPE_NFREQ, nctx), F32)
    c_cols = jnp.zeros((d, 128), F32).at[:, 0].set(c[0]).at[:, 1].set(c_ctx)

    for l in range(depth):
        last = l == depth - 1
        mod = _modulation(c_cols, w_mod, b_mod, l)

        w_in_l = _layout_w_in(w_in[l])
        kv_up = w_kv_up[l].reshape(MLA_KV_RANK, MLA_HEADS, MLA_NOPE + MLA_V)
        wknT = kv_up[:, :, :MLA_NOPE].reshape(MLA_KV_RANK, -1).T.astype(BF16)
        wvT = kv_up[:, :, MLA_NOPE:].reshape(MLA_KV_RANK, -1).T.astype(BF16)
        gains = (na_q_g[l].reshape(-1, 1), na_k_g[l].reshape(-1, 1), mla_kv_g[l].reshape(-1, 1),
                 _pad_rows(mla_q_g[l].reshape(-1, 1), MLA_QPAD), _pad_rows(mla_k_g[l].reshape(-1, 1), MLA_QPAD))
        bias = _na_bias_tables(na_rel_bias[l], rows)
        conv_w8 = _pad_rows(conv_w[l], 8)
        ffn_cw8 = _pad_rows(jnp.concatenate([ffn_conv_w[l], ffn_conv_b[l][None]], axis=0), 8)
        wa, wb, wc, wo = (w.astype(BF16) for w in (w_br_a[l], w_br_b[l], w_br_c[l], w_o[l]))
        w_up_l = w_up[l].astype(BF16)
        w_down_l = w_down[l].astype(BF16)

        pc, gc = _in_proj(cs, norm1_g[l], mod, 0, 1, w_in_l, nctx)
        naqcT, nakc, navcT, mqcT, mkc, mvcT, qnc, knc = _prep(pc, cos_c, sin_c, gains, wknT, wvT, nctx)
        px, gx = _in_proj(xs, norm1_g[l], mod, 0, 0, w_in_l, tm_x)
        naqT, nak, navT, mqT, mk, mvT, qn, kn = _prep(px, cos_x, sin_x, gains, wknT, wvT, tm_half)

        ya = _na_latent(naqT, nak, navT, nakc, navcT, bias)
        yb = _mla_attention(mqT, qn, mk, mvT, kn, (mkc, mvcT, knc), tq_mla, tk_mla)
        xs = _merge(ya, yb, px, gx, xs, mod, 2, 0, conv_w8, wa, wb, wc, wo, tm_merge)
        av = _ffn_up(xs, norm2_g[l], mod, 3, 0, w_up_l, tm_x)
        xs = _ffn_down(av, xs, mod, 5, 0, ffn_cw8, w_down_l, tm_merge)

        if not last:
            yac = _na_ctx(naqcT, nakc, navcT)
            ybc = _mla_attention(mqcT, qnc, mkc, mvcT, knc, None, nctx, nctx)
            cs = _merge(yac, ybc, pc, gc, cs, mod, 2, 1, conv_w8, wa, wb, wc, wo, nctx)
            avc = _ffn_up(cs, norm2_g[l], mod, 3, 1, w_up_l, nctx)
            cs = _ffn_down(avc, cs, mod, 5, 1, ffn_cw8, w_down_l, nctx)

    return xs[None]
```

```python
import functools

import math

import jax
import jax.numpy as jnp
import numpy as np
from jax import lax
from jax.experimental import pallas as pl
from jax.experimental.pallas import tpu as pltpu

F32 = jnp.float32
BF16 = jnp.bfloat16

D_MODEL = 2048
GRID_W = 64
EPS = 1e-6
NA_HEADS = 8
NA_HEAD_DIM = 64
NA_WIN_H = 8
NA_WIN_W = 16
NA_W = NA_HEADS * NA_HEAD_DIM
MLA_HEADS = 4
MLA_NOPE = 128
MLA_ROPE = 64
MLA_V = 128
MLA_KV_RANK = 256
MLA_QK = MLA_NOPE + MLA_ROPE
MLA_QPAD = 256
ROPE_NFREQ = MLA_ROPE // 4
ROPE_BASE = 10000.0
CONV_CH = 512
D_FF = 5632

R_NA_K = 0
R_NA_V = R_NA_K + NA_W
R_CKV = R_NA_V + NA_W
R_KROPE = R_CKV + MLA_KV_RANK
R_NA_Q = R_KROPE + MLA_ROPE
R_MLA_Q = R_NA_Q + NA_W
R_CONV_U = R_MLA_Q + MLA_HEADS * MLA_QK
R_GATE = R_CONV_U + 3 * CONV_CH

P_NA_K = 0
P_NA_V = 512
P_NA_Q = 1024
P_MLA_Q = 1536
P_CKV = 2560
P_KROPE = 2816
P_QKV_END = 3072
P_UBC = 3072
P_COLS = 4608
G_COLS = 3 * D_MODEL
PG_P = G_COLS

NA_QROWS = 4
NA_KROWS = 12
NA_TQ = NA_QROWS * GRID_W
NA_TK = NA_KROWS * GRID_W
NEG = -1e30
LOG2E = math.log2(math.e)
MLA_BOUND_MARGIN = 1.02
MLA_MAX_SHIFT = 50.0

VMEM_LIMIT = 56 * 1024 * 1024


def _cparams(sem):
    return pltpu.CompilerParams(dimension_semantics=sem, vmem_limit_bytes=VMEM_LIMIT)


def _const_spec(shape):
    nd = len(shape)
    return pl.BlockSpec(shape, lambda *_: (0,) * nd, pipeline_mode=pl.Buffered(1))


def _layer_spec(shape, layer):
    nd = len(shape)
    return pl.BlockSpec((None,) + tuple(shape), lambda *_: (layer,) + (0,) * nd, pipeline_mode=pl.Buffered(1))


MOD_KC = 256


def _mod_kernel(c_ref, w_ref, b_ref, o_ref):
    tn = o_ref.shape[1]
    nk = w_ref.shape[0] // MOD_KC

    def body(kc, acc):
        k0 = pl.multiple_of(kc * MOD_KC, MOD_KC)
        w = w_ref[pl.ds(k0, MOD_KC), :]
        c = c_ref[pl.ds(k0, MOD_KC), :]
        s = c * jax.nn.sigmoid(c)
        out = []
        for r in range(2):
            prod = s[:, r:r + 1] * w
            out.append(acc[r] + jnp.sum(prod.reshape(MOD_KC // 8, 8, tn), axis=0))
        return tuple(out)

    z = jnp.zeros((8, tn), F32)
    acc = lax.fori_loop(0, nk, body, (z, z))
    r0, r1 = [jnp.sum(a, axis=0, keepdims=True) + b_ref[...] for a in acc]
    ridx = lax.broadcasted_iota(jnp.int32, (8, tn), 0)
    o_ref[...] = jnp.where(ridx == 0, r0, jnp.where(ridx == 1, r1, 0.0))


def _modulation(c_cols, w_mod, b_mod, layer):
    depth, d, n = w_mod.shape
    tn = 1536
    return pl.pallas_call(
        _mod_kernel,
        grid=(n // tn,),
        in_specs=[
            _const_spec((d, 128)),
            pl.BlockSpec((None, d, tn), lambda j: (layer, 0, j)),
            pl.BlockSpec((None, 1, tn), lambda j: (layer, 0, j)),
        ],
        out_specs=pl.BlockSpec((8, tn), lambda j: (0, j)),
        out_shape=jax.ShapeDtypeStruct((8, n), F32),
        compiler_params=_cparams(("arbitrary",)),
        name="modulation",
    )(c_cols, w_mod, b_mod.reshape(depth, 1, n))


NORM_RC = 128
IN_TN = 1536
UP_TN = 1024


def _norm_rows(x_ref, g_ref, sh_ref, sc_ref, hx_ref, row):
    tm = x_ref.shape[0]
    g = g_ref[...]
    sc1 = 1.0 + sc_ref[row:row + 1, :]
    sh = sh_ref[row:row + 1, :]

    def body(r, carry):
        r0 = pl.multiple_of(r * NORM_RC, NORM_RC)
        x = x_ref[pl.ds(r0, NORM_RC), :]
        ms = jnp.mean(x * x, axis=-1, keepdims=True)
        y = (x * lax.rsqrt(ms + EPS)) * g
        hx_ref[pl.ds(r0, NORM_RC), :] = (y * sc1 + sh).astype(BF16)
        return carry

    lax.fori_loop(0, tm // NORM_RC, body, 0)


def _norm_matmul_kernel(x_ref, g_ref, sh_ref, sc_ref, w_ref, o_ref, hx_ref, *, row):
    @pl.when(pl.program_id(1) == 0)
    def _():
        _norm_rows(x_ref, g_ref, sh_ref, sc_ref, hx_ref, row)

    o_ref[...] = jnp.dot(hx_ref[...], w_ref[...], preferred_element_type=F32).astype(BF16)


def _norm_matmul(x2d, norm_g, mod, sh_blk, row, w_all, layer, tm, tn, name):
    n, d = x2d.shape
    ncol = w_all.shape[2]
    return pl.pallas_call(
        functools.partial(_norm_matmul_kernel, row=row),
        grid=(n // tm, ncol // tn),
        in_specs=[
            pl.BlockSpec((tm, d), lambda i, j: (i, 0)),
            _layer_spec((1, d), layer),
            pl.BlockSpec((8, d), lambda i, j: (0, sh_blk), pipeline_mode=pl.Buffered(1)),
            pl.BlockSpec((8, d), lambda i, j: (0, sh_blk + 1), pipeline_mode=pl.Buffered(1)),
            pl.BlockSpec((None, d, tn), lambda i, j: (layer, 0, j)),
        ],
        out_specs=pl.BlockSpec((tm, tn), lambda i, j: (i, j)),
        out_shape=jax.ShapeDtypeStruct((n, ncol), BF16),
        scratch_shapes=[pltpu.VMEM((tm, d), BF16)],
        compiler_params=_cparams(("parallel", "arbitrary")),
        name=name,
    )(x2d, norm_g, mod, mod, w_all)


def _rope_rows(r, cos_ref, sin_ref):
    out = []
    for part in range(2):
        cs = cos_ref[part]
        sn = sin_ref[part]
        x1 = r[32 * part:32 * part + 16]
        x2 = r[32 * part + 16:32 * part + 32]
        out.append(x1 * cs - x2 * sn)
        out.append(x2 * cs + x1 * sn)
    return out


def _prep_kernel(p_ref, cos_ref, sin_ref, naqg_ref, nakg_ref, kvg_ref, mqg_ref, mkg_ref,
                 wkn_ref, wv_ref,
                 naqT_ref, nak_ref, navT_ref, mqT_ref, mk_ref, mvT_ref, qn_ref, kn_ref):
    tm = p_ref.shape[0]

    def head_norm_T(cols, g_ref, scale):
        xt = p_ref[:, cols:cols + NA_W].astype(F32).T
        x3 = xt.reshape(NA_HEADS, NA_HEAD_DIM, tm)
        ms = jnp.mean(x3 * x3, axis=1, keepdims=True)
        y = x3 * lax.rsqrt(ms + EPS) * g_ref[...][None]
        if scale != 1.0:
            y = y * scale
        return y.reshape(NA_W, tm)

    naqT_ref[...] = head_norm_T(P_NA_Q, naqg_ref, NA_HEAD_DIM ** -0.5 * LOG2E).astype(BF16)
    nak_ref[...] = head_norm_T(P_NA_K, nakg_ref, 1.0).T.astype(BF16)
    navT_ref[...] = p_ref[:, P_NA_V:P_NA_V + NA_W].astype(F32).T.astype(BF16)

    mqg = mqg_ref[...]
    qscale = MLA_QK ** -0.5 * LOG2E
    for h in range(MLA_HEADS):
        c0 = P_MLA_Q + h * MLA_QPAD
        qt = p_ref[:, c0:c0 + MLA_QPAD].astype(F32).T
        ms = jnp.sum(qt * qt, axis=0, keepdims=True) * (1.0 / MLA_QK)
        y = qt * lax.rsqrt(ms + EPS) * mqg
        r0 = h * MLA_QPAD
        mqT_ref[r0:r0 + MLA_NOPE, :] = (y[:MLA_NOPE] * qscale).astype(BF16)
        parts = _rope_rows(y[MLA_NOPE:MLA_QK], cos_ref, sin_ref)
        for k, part in enumerate(parts):
            a = r0 + MLA_NOPE + 16 * k
            mqT_ref[a:a + 16, :] = (part * qscale).astype(BF16)
        mqT_ref[r0 + MLA_QK:r0 + MLA_QPAD, :] = jnp.zeros((MLA_QPAD - MLA_QK, tm), BF16)
        qnorm = jnp.sqrt(jnp.sum(y * y, axis=0, keepdims=True)) * qscale
        qn_ref[8 * h:8 * h + 8, :] = jnp.broadcast_to(qnorm, (8, tm))

    ct = p_ref[:, P_CKV:P_CKV + MLA_KV_RANK].astype(F32).T
    ms = jnp.mean(ct * ct, axis=0, keepdims=True)
    cn = (ct * lax.rsqrt(ms + EPS) * kvg_ref[...]).astype(BF16)
    mvT_ref[...] = jnp.dot(wv_ref[...], cn, preferred_element_type=F32).astype(BF16)
    knT = jnp.dot(wkn_ref[...], cn, preferred_element_type=F32)
    krT = p_ref[:, P_KROPE:P_KROPE + 128].astype(F32).T[:MLA_ROPE]
    kr_ss = jnp.sum(krT * krT, axis=0, keepdims=True)
    mkg = mkg_ref[...]
    pad = (lax.broadcasted_iota(jnp.int32, (MLA_QPAD - MLA_QK, tm), 0) == 0).astype(F32)
    for h in range(MLA_HEADS):
        kn = knT[h * MLA_NOPE:(h + 1) * MLA_NOPE]
        ms = (jnp.sum(kn * kn, axis=0, keepdims=True) + kr_ss) * (1.0 / MLA_QK)
        rinv = lax.rsqrt(ms + EPS)
        kn = kn * rinv * mkg[:MLA_NOPE]
        kr = krT * rinv * mkg[MLA_NOPE:MLA_QK]
        knorm = jnp.sqrt(jnp.sum(kn * kn, axis=0, keepdims=True) + jnp.sum(kr * kr, axis=0, keepdims=True))
        kn_ref[8 * h:8 * h + 8, :] = jnp.broadcast_to(knorm, (8, tm))
        pieces = [kn] + _rope_rows(kr, cos_ref, sin_ref) + [pad]
        kT = jnp.concatenate(pieces, axis=0)
        mk_ref[:, h * MLA_QPAD:(h + 1) * MLA_QPAD] = kT.T.astype(BF16)


def _prep(pg, cosT, sinT, gains, wknT, wvT, layer, tm):
    n = pg.shape[0]
    naqg, nakg, kvg, mqg, mkg = gains
    return pl.pallas_call(
        _prep_kernel,
        grid=(n // tm,),
        in_specs=[
            pl.BlockSpec((tm, P_QKV_END), lambda i: (i, PG_P // P_QKV_END)),
            pl.BlockSpec((2, ROPE_NFREQ, tm), lambda i: (0, 0, i)),
            pl.BlockSpec((2, ROPE_NFREQ, tm), lambda i: (0, 0, i)),
            _layer_spec((NA_HEAD_DIM, 1), layer),
            _layer_spec((NA_HEAD_DIM, 1), layer),
            _layer_spec((MLA_KV_RANK, 1), layer),
            _layer_spec((MLA_QPAD, 1), layer),
            _layer_spec((MLA_QPAD, 1), layer),
            _layer_spec((MLA_HEADS * MLA_NOPE, MLA_KV_RANK), layer),
            _layer_spec((MLA_HEADS * MLA_V, MLA_KV_RANK), layer),
        ],
        out_specs=[
            pl.BlockSpec((NA_W, tm), lambda i: (0, i)),
            pl.BlockSpec((tm, NA_W), lambda i: (i, 0)),
            pl.BlockSpec((NA_W, tm), lambda i: (0, i)),
            pl.BlockSpec((MLA_HEADS * MLA_QPAD, tm), lambda i: (0, i)),
            pl.BlockSpec((tm, MLA_HEADS * MLA_QPAD), lambda i: (i, 0)),
            pl.BlockSpec((MLA_HEADS * MLA_V, tm), lambda i: (0, i)),
            pl.BlockSpec((MLA_HEADS * 8, tm), lambda i: (0, i)),
            pl.BlockSpec((MLA_HEADS * 8, tm), lambda i: (0, i)),
        ],
        out_shape=[
            jax.ShapeDtypeStruct((NA_W, n), BF16),
            jax.ShapeDtypeStruct((n, NA_W), BF16),
            jax.ShapeDtypeStruct((NA_W, n), BF16),
            jax.ShapeDtypeStruct((MLA_HEADS * MLA_QPAD, n), BF16),
            jax.ShapeDtypeStruct((n, MLA_HEADS * MLA_QPAD), BF16),
            jax.ShapeDtypeStruct((MLA_HEADS * MLA_V, n), BF16),
            jax.ShapeDtypeStruct((MLA_HEADS * 8, n), F32),
            jax.ShapeDtypeStruct((MLA_HEADS * 8, n), F32),
        ],
        compiler_params=_cparams(("parallel",)),
        name="qkv_prep",
    )(pg, cosT, sinT, naqg, nakg, kvg, mqg, mkg, wknT, wvT)


def _na_heads(q_ref, k_refs, vT_refs, bias_ref, o_ref):
    tq = q_ref.shape[1]
    upper = lax.broadcasted_iota(jnp.int32, (128, tq), 0) >= NA_HEAD_DIM
    outs = []
    for h in range(NA_HEADS):
        j, e = divmod(h, 2)
        q2 = q_ref[128 * j:128 * j + 128, :]
        qm = jnp.where(upper == bool(e), q2, jnp.zeros_like(q2))
        s_list = []
        off = 0
        for b, k_ref in enumerate(k_refs):
            s = jnp.dot(k_ref[:, 128 * j:128 * j + 128], qm, preferred_element_type=F32)
            nk = k_ref.shape[0]
            if bias_ref is not None and b < len(k_refs) - 1:
                s = s + bias_ref[0, h, off:off + nk, :]
            off += nk
            s_list.append(s)
        m = s_list[0].max(axis=0, keepdims=True)
        for s in s_list[1:]:
            m = jnp.maximum(m, s.max(axis=0, keepdims=True))
        l = jnp.zeros((1, tq), F32)
        acc = jnp.zeros((NA_HEAD_DIM, tq), F32)
        for s, vT_ref in zip(s_list, vT_refs):
            p = jnp.exp2(s - m)
            l = l + p.sum(axis=0, keepdims=True)
            acc = acc + jnp.dot(vT_ref[NA_HEAD_DIM * h:NA_HEAD_DIM * (h + 1), :], p.astype(BF16),
                                preferred_element_type=F32)
        outs.append(acc / l)
    o_ref[...] = jnp.concatenate(outs, axis=0).T.astype(BF16)


def _na_latent_kernel(q_ref, k0, k1, k2, kc, v0, v1, v2, vc, bias_ref, o_ref):
    _na_heads(q_ref, [k0, k1, k2, kc], [v0, v1, v2, vc], bias_ref, o_ref)


def _na_ctx_kernel(q_ref, kc, vc, o_ref):
    _na_heads(q_ref, [kc], [vc], None, o_ref)


def _na_latent(naqT, nak, navT, nakc, navcT, bias, layer):
    n = naqT.shape[1]
    nb = n // NA_TQ
    nctx = nakc.shape[0]

    def kb(o):
        return lambda i: (jnp.clip(i - 1, 0, nb - 3) + o, 0)

    def vb(o):
        return lambda i: (0, jnp.clip(i - 1, 0, nb - 3) + o)

    def variant(i):
        return (layer, jnp.where(i == 0, 0, jnp.where(i == nb - 1, 2, 1)), 0, 0, 0)

    return pl.pallas_call(
        _na_latent_kernel,
        grid=(nb,),
        in_specs=[
            pl.BlockSpec((NA_W, NA_TQ), lambda i: (0, i)),
            pl.BlockSpec((NA_TQ, NA_W), kb(0)),
            pl.BlockSpec((NA_TQ, NA_W), kb(1)),
            pl.BlockSpec((NA_TQ, NA_W), kb(2)),
            _const_spec((nctx, NA_W)),
            pl.BlockSpec((NA_W, NA_TQ), vb(0)),
            pl.BlockSpec((NA_W, NA_TQ), vb(1)),
            pl.BlockSpec((NA_W, NA_TQ), vb(2)),
            _const_spec((NA_W, nctx)),
            pl.BlockSpec((None, 1, NA_HEADS, NA_TK, NA_TQ), variant),
        ],
        out_specs=pl.BlockSpec((NA_TQ, NA_W), lambda i: (i, 0)),
        out_shape=jax.ShapeDtypeStruct((n, NA_W), BF16),
        compiler_params=_cparams(("parallel",)),
        name="na_latent",
    )(naqT, nak, nak, nak, nakc, navT, navT, navT, navcT, bias)


def _na_ctx(naqcT, nakc, navcT):
    nctx = nakc.shape[0]
    return pl.pallas_call(
        _na_ctx_kernel,
        grid=(1,),
        in_specs=[_const_spec((NA_W, nctx)), _const_spec((nctx, NA_W)), _const_spec((NA_W, nctx))],
        out_specs=pl.BlockSpec((nctx, NA_W), lambda i: (0, 0)),
        out_shape=jax.ShapeDtypeStruct((nctx, NA_W), BF16),
        compiler_params=_cparams(("arbitrary",)),
        name="na_ctx",
    )(naqcT, nakc, navcT)


def _na_bias_tables(rel_bias, rows):
    ndr = 2 * NA_WIN_H - 1
    ndc = 2 * NA_WIN_W - 1
    col = np.arange(GRID_W)
    start_c = np.clip(col - NA_WIN_W // 2, 0, GRID_W - NA_WIN_W)
    in_col = (col[:, None] >= start_c[None, :]) & (col[:, None] < start_c[None, :] + NA_WIN_W)
    dc = col[:, None] - col[None, :] + (NA_WIN_W - 1)
    sel_c = (dc[:, :, None] == np.arange(ndc)).astype(np.float32)
    sel_r, ok = [], []
    for r0, base in ((0, 0), (NA_QROWS, 0), (rows - NA_QROWS, rows - NA_KROWS)):
        qr = r0 + np.arange(NA_QROWS)
        kr = base + np.arange(NA_KROWS)
        start_r = np.clip(qr - NA_WIN_H // 2, 0, rows - NA_WIN_H)
        in_row = (kr[:, None] >= start_r[None, :]) & (kr[:, None] < start_r[None, :] + NA_WIN_H)
        dr = kr[:, None] - qr[None, :] + (NA_WIN_H - 1)
        sel_r.append((dr[:, :, None] == np.arange(ndr)).astype(np.float32))
        ok.append(in_row[:, None, :, None] & in_col[None, :, None, :])
    sel_r = np.stack(sel_r)
    ok = np.stack(ok).reshape(3, 1, NA_TK, NA_TQ)
    hp = lax.Precision.HIGHEST
    by_col = jnp.einsum('lhrc,kqc->lhrkq', rel_bias.astype(F32), sel_c, precision=hp)
    tab = jnp.einsum('lhrkq,vabr->lvhakbq', by_col, sel_r, precision=hp)
    tab = tab.reshape(-1, 3, NA_HEADS, NA_TK, NA_TQ) * LOG2E
    return jnp.where(ok, tab, NEG)


def _mla_kernel(*refs, tk, has_ctx):
    if has_ctx:
        qT_ref, qn_ref, k_ref, vT_ref, kn_ref, kc_ref, vcT_ref, knc_ref, o_ref = refs[:9]
    else:
        qT_ref, qn_ref, k_ref, vT_ref, kn_ref, o_ref = refs[:6]
    m_ref, l_ref, acc_ref, al_ref, s_a, s_b, p_a, p_b, qs_ref = refs[-9:]
    tq = qT_ref.shape[1]
    nk = k_ref.shape[0] // tk

    kmax = jnp.max(kn_ref[0:1, :])
    if has_ctx:
        kmax = jnp.maximum(kmax, jnp.max(knc_ref[0:1, :]))
    bound = qn_ref[0:1, :] * (kmax * MLA_BOUND_MARGIN)
    shift_ok = jnp.max(bound) <= MLA_MAX_SHIFT

    @pl.when(shift_ok)
    def _():
        qs_ref[...] = qT_ref[...]
        row0 = lax.broadcasted_iota(jnp.int32, (16, tq), 0) == 0
        qs_ref[MLA_QK:MLA_QK + 16, :] = jnp.where(row0, -bound, 0.0).astype(BF16)

        def chunk(k_blk, vT_blk):
            p = jnp.exp2(jnp.dot(k_blk, qs_ref[...], preferred_element_type=F32))
            return p.sum(axis=0, keepdims=True), jnp.dot(vT_blk, p.astype(BF16), preferred_element_type=F32)

        if has_ctx:
            l0, a0 = chunk(kc_ref[...], vcT_ref[...])
            l_ref[...] = l0
            acc_ref[...] = a0
        else:
            l_ref[...] = jnp.zeros((1, tq), F32)
            acc_ref[...] = jnp.zeros((MLA_V, tq), F32)
        unroll = next(u for u in (8, 4, 2, 1) if nk % u == 0)

        def body(it, carry):
            l_sum = l_ref[...]
            a_sum = acc_ref[...]
            for u in range(unroll):
                k0 = pl.multiple_of((it * unroll + u) * tk, tk)
                lp, ap = chunk(k_ref[pl.ds(k0, tk), :], vT_ref[:, pl.ds(k0, tk)])
                l_sum = l_sum + lp
                a_sum = a_sum + ap
            l_ref[...] = l_sum
            acc_ref[...] = a_sum
            return carry

        lax.fori_loop(0, nk // unroll, body, 0)

    @pl.when(jnp.logical_not(shift_ok))
    def _():
        if has_ctx:
            s = jnp.dot(kc_ref[...], qT_ref[...], preferred_element_type=F32)
            m = s.max(axis=0, keepdims=True)
            p = jnp.exp2(s - m)
            m_ref[...] = m
            l_ref[...] = p.sum(axis=0, keepdims=True)
            acc_ref[...] = jnp.dot(vcT_ref[...], p.astype(BF16), preferred_element_type=F32)
        else:
            m_ref[...] = jnp.full((1, tq), NEG, F32)
            l_ref[...] = jnp.zeros((1, tq), F32)
            acc_ref[...] = jnp.zeros((MLA_V, tq), F32)

        def scores(chunk, s_ref):
            k0 = pl.multiple_of(chunk * tk, tk)
            s_ref[...] = jnp.dot(k_ref[pl.ds(k0, tk), :], qT_ref[...], preferred_element_type=F32)

        def softmax(s_ref, p_ref):
            s = s_ref[...]
            m_old = m_ref[...]
            m_new = jnp.maximum(m_old, s.max(axis=0, keepdims=True))
            alpha = jnp.exp2(m_old - m_new)
            p = jnp.exp2(s - m_new)
            l_ref[...] = alpha * l_ref[...] + p.sum(axis=0, keepdims=True)
            m_ref[...] = m_new
            p_ref[...] = p.astype(BF16)
            return alpha

        def values(chunk, p_ref, alpha):
            k0 = pl.multiple_of(chunk * tk, tk)
            acc_ref[...] = alpha * acc_ref[...] + jnp.dot(vT_ref[:, pl.ds(k0, tk)], p_ref[...],
                                                          preferred_element_type=F32)

        if nk % 2 == 0:
            scores(0, s_a)
            p_b[...] = jnp.zeros(p_b.shape, BF16)
            al_ref[...] = jnp.ones((1, tq), F32)

            def body(it, carry):
                c0 = 2 * it
                scores(c0 + 1, s_b)
                a0 = softmax(s_a, p_a)
                values(jnp.maximum(c0 - 1, 0), p_b, al_ref[...])
                scores(jnp.minimum(c0 + 2, nk - 1), s_a)
                a1 = softmax(s_b, p_b)
                values(c0, p_a, a0)
                al_ref[...] = a1
                return carry

            lax.fori_loop(0, nk // 2, body, 0)
            values(nk - 1, p_b, al_ref[...])
        else:
            for c in range(nk):
                scores(c, s_a)
                values(c, p_a, softmax(s_a, p_a))

    o_ref[...] = (acc_ref[...] / l_ref[...]).T.astype(BF16)


def _mla_attention(mqT, qn, mk, mvT, kn, ctx_kv, tq, tk):
    nq = mqT.shape[1]
    nkeys = mk.shape[0]
    has_ctx = ctx_kv is not None
    in_specs = [
        pl.BlockSpec((MLA_QPAD, tq), lambda h, i: (h, i)),
        pl.BlockSpec((8, tq), lambda h, i: (h, i)),
        pl.BlockSpec((nkeys, MLA_QPAD), lambda h, i: (0, h)),
        pl.BlockSpec((MLA_V, nkeys), lambda h, i: (h, 0)),
        pl.BlockSpec((8, nkeys), lambda h, i: (h, 0)),
    ]
    args = [mqT, qn, mk, mvT, kn]
    if has_ctx:
        kc, vcT, knc = ctx_kv
        nctx = kc.shape[0]
        in_specs += [
            pl.BlockSpec((nctx, MLA_QPAD), lambda h, i: (0, h)),
            pl.BlockSpec((MLA_V, nctx), lambda h, i: (h, 0)),
            pl.BlockSpec((8, nctx), lambda h, i: (h, 0)),
        ]
        args += [kc, vcT, knc]
    return pl.pallas_call(
        functools.partial(_mla_kernel, tk=tk, has_ctx=has_ctx),
        grid=(MLA_HEADS, nq // tq),
        in_specs=in_specs,
        out_specs=pl.BlockSpec((tq, MLA_V), lambda h, i: (i, h)),
        out_shape=jax.ShapeDtypeStruct((nq, MLA_HEADS * MLA_V), BF16),
        scratch_shapes=[pltpu.VMEM((1, tq), F32), pltpu.VMEM((1, tq), F32), pltpu.VMEM((MLA_V, tq), F32),
                        pltpu.VMEM((1, tq), F32),
                        pltpu.VMEM((tk, tq), F32), pltpu.VMEM((tk, tq), F32),
                        pltpu.VMEM((tk, tq), BF16), pltpu.VMEM((tk, tq), BF16),
                        pltpu.VMEM((MLA_QPAD, tq), BF16)],
        compiler_params=_cparams(("parallel", "arbitrary")),
        name="mla_attention",
    )(*args)


def _conv3_rows(x, prev_row, next_row, w):
    tm = x.shape[0]
    rows = lax.broadcasted_iota(jnp.int32, (tm, 1), 0)
    xm1 = jnp.where(rows == 0, prev_row, pltpu.roll(x, 1, 0))
    xp1 = jnp.where(rows == tm - 1, next_row, pltpu.roll(x, tm - 1, 0))
    return w[0:1, :] * xm1 + w[1:2, :] * x + w[2:3, :] * xp1


HALO = 16


def _merge_kernel(ya_ref, yb_ref, ubc_ref, prev_ref, next_ref, gate_ref, x_ref, g1_ref, cw_ref,
                  wa_ref, wb_ref, wc_ref, wo_ref, o_ref, *, row):
    i = pl.program_id(0)
    nblk = pl.num_programs(0)
    ubc = ubc_ref[...].astype(F32)
    u = ubc[:, :CONV_CH]
    gb = ubc[:, CONV_CH:2 * CONV_CH]
    gc = ubc[:, 2 * CONV_CH:]
    cu = gc * u
    pr = prev_ref[...].astype(F32)[HALO - 1:HALO, :]
    nx = next_ref[...].astype(F32)[0:1, :]
    cu_prev = pr[:, 2 * CONV_CH:] * pr[:, :CONV_CH] * (i > 0).astype(F32)
    cu_next = nx[:, 2 * CONV_CH:] * nx[:, :CONV_CH] * (i < nblk - 1).astype(F32)
    yc = (gb * _conv3_rows(cu, cu_prev, cu_next, cw_ref[...])).astype(BF16)

    a = jnp.dot(ya_ref[...], wa_ref[...], preferred_element_type=F32)
    b = jnp.dot(yb_ref[...], wb_ref[...], preferred_element_type=F32)
    c = jnp.dot(yc, wc_ref[...], preferred_element_type=F32)
    d = D_MODEL
    merged = (jax.nn.sigmoid(gate_ref[:, :d].astype(F32)) * a
              + jax.nn.sigmoid(gate_ref[:, d:2 * d].astype(F32)) * b
              + jax.nn.sigmoid(gate_ref[:, 2 * d:].astype(F32)) * c)
    mix = jnp.dot(merged.astype(BF16), wo_ref[...], preferred_element_type=F32)
    o_ref[...] = x_ref[...] + g1_ref[row:row + 1, :] * mix


def _merge(ya, yb, pg, x2d, mod, g1_blk, row, conv_w8, wa, wb, wc, wo, layer, tm):
    n, d = x2d.shape
    ubc_blk = (PG_P + P_UBC) // (3 * CONV_CH)
    hb = tm // HALO
    last = n // HALO - 1
    return pl.pallas_call(
        functools.partial(_merge_kernel, row=row),
        grid=(n // tm,),
        in_specs=[
            pl.BlockSpec((tm, NA_W), lambda i: (i, 0)),
            pl.BlockSpec((tm, MLA_HEADS * MLA_V), lambda i: (i, 0)),
            pl.BlockSpec((tm, 3 * CONV_CH), lambda i: (i, ubc_blk)),
            pl.BlockSpec((HALO, 3 * CONV_CH), lambda i: (jnp.maximum(i * hb - 1, 0), ubc_blk)),
            pl.BlockSpec((HALO, 3 * CONV_CH), lambda i: (jnp.minimum((i + 1) * hb, last), ubc_blk)),
            pl.BlockSpec((tm, G_COLS), lambda i: (i, 0)),
            pl.BlockSpec((tm, d), lambda i: (i, 0)),
            pl.BlockSpec((8, d), lambda i: (0, g1_blk), pipeline_mode=pl.Buffered(1)),
            _layer_spec((8, CONV_CH), layer),
            _layer_spec(wa.shape[1:], layer),
            _layer_spec(wb.shape[1:], layer),
            _layer_spec(wc.shape[1:], layer),
            _layer_spec(wo.shape[1:], layer),
        ],
        out_specs=pl.BlockSpec((tm, d), lambda i: (i, 0)),
        out_shape=jax.ShapeDtypeStruct((n, d), F32),
        compiler_params=_cparams(("parallel",)),
        name="merge_out",
    )(ya, yb, pg, pg, pg, pg, x2d, mod, conv_w8, wa, wb, wc, wo)


FFN_KC = 512


def _ffn_down_kernel(a_ref, prev_ref, next_ref, val_ref, cw_ref, wd_ref, x_ref, g2_ref, o_ref,
                     acc_ref, h_a, h_b, *, row):
    i = pl.program_id(0)
    nblk = pl.num_programs(0)
    has_prev = (i > 0).astype(F32)
    has_next = (i < nblk - 1).astype(F32)
    nchunks = D_FF // FFN_KC

    def hidden(c, h_ref):
        sl = slice(c * FFN_KC, (c + 1) * FFN_KC)
        w = cw_ref[:, sl]
        a = a_ref[:, sl].astype(F32)
        pr = prev_ref[:, sl].astype(F32)[HALO - 1:HALO, :] * has_prev
        nx = next_ref[:, sl].astype(F32)[0:1, :] * has_next
        a = _conv3_rows(a, pr, nx, w) + w[3:4, :]
        h_ref[...] = (a * jax.nn.sigmoid(a) * val_ref[:, sl].astype(F32)).astype(BF16)

    bufs = (h_a, h_b)
    hidden(0, h_a)
    for c in range(nchunks):
        if c + 1 < nchunks:
            hidden(c + 1, bufs[(c + 1) % 2])
        part = jnp.dot(bufs[c % 2][...], wd_ref[c * FFN_KC:(c + 1) * FFN_KC, :], preferred_element_type=F32)
        if c == 0:
            acc_ref[...] = part
        else:
            acc_ref[...] += part
    o_ref[...] = x_ref[...] + g2_ref[row:row + 1, :] * acc_ref[...]


def _ffn_down(av, x2d, mod, g2_blk, row, cw8, wd, layer, tm):
    n, d = x2d.shape
    hb = tm // HALO
    last = n // HALO - 1
    return pl.pallas_call(
        functools.partial(_ffn_down_kernel, row=row),
        grid=(n // tm,),
        in_specs=[
            pl.BlockSpec((tm, D_FF), lambda i: (i, 0)),
            pl.BlockSpec((HALO, D_FF), lambda i: (jnp.maximum(i * hb - 1, 0), 0)),
            pl.BlockSpec((HALO, D_FF), lambda i: (jnp.minimum((i + 1) * hb, last), 0)),
            pl.BlockSpec((tm, D_FF), lambda i: (i, 1)),
            _layer_spec((8, D_FF), layer),
            _layer_spec((D_FF, d), layer),
            pl.BlockSpec((tm, d), lambda i: (i, 0)),
            pl.BlockSpec((8, d), lambda i: (0, g2_blk), pipeline_mode=pl.Buffered(1)),
        ],
        out_specs=pl.BlockSpec((tm, d), lambda i: (i, 0)),
        out_shape=jax.ShapeDtypeStruct((n, d), F32),
        scratch_shapes=[pltpu.VMEM((tm, d), F32), pltpu.VMEM((tm, FFN_KC), BF16), pltpu.VMEM((tm, FFN_KC), BF16)],
        compiler_params=_cparams(("parallel",)),
        name="ffn_down",
    )(av, av, av, av, cw8, wd, x2d, mod)


def _layout_w_in(w):
    z64 = jnp.zeros(w.shape[:-1] + (64,), w.dtype)
    cols = [w[..., R_GATE:], w[..., R_NA_K:R_NA_V], w[..., R_NA_V:R_CKV], w[..., R_NA_Q:R_MLA_Q]]
    for h in range(MLA_HEADS):
        cols += [w[..., R_MLA_Q + h * MLA_QK:R_MLA_Q + (h + 1) * MLA_QK], z64]
    cols += [w[..., R_CKV:R_KROPE], w[..., R_KROPE:R_NA_Q], z64, z64, z64, w[..., R_CONV_U:R_GATE]]
    return jnp.concatenate(cols, axis=-1).astype(BF16)


def _pad_rows(a, rows):
    return jnp.concatenate([a, jnp.zeros((a.shape[0], rows - a.shape[1], a.shape[2]), a.dtype)], axis=1)


def _rope_tables_T(n_tok):
    t = jnp.arange(n_tok)
    row = (t // GRID_W).astype(F32)
    col = (t % GRID_W).astype(F32)
    inv = ROPE_BASE ** (-jnp.arange(ROPE_NFREQ, dtype=F32) / ROPE_NFREQ)
    ang = jnp.stack([inv[:, None] * row[None, :], inv[:, None] * col[None, :]], axis=0)
    return jnp.cos(ang), jnp.sin(ang)


def kernel(x, c, ctx, c_ctx, w_mod, b_mod, norm1_g, w_in, na_q_g, na_k_g, na_rel_bias, mla_kv_g, w_kv_up,
           mla_q_g, mla_k_g, conv_w, w_br_a, w_br_b, w_br_c, w_o, norm2_g, w_up, ffn_conv_w, ffn_conv_b, w_down):
    bsz, seq, d = x.shape
    nctx = ctx.shape[1]
    depth = w_mod.shape[0]
    rows = seq // GRID_W
    assert bsz == 1 and d == D_MODEL and seq % NA_TQ == 0 and rows >= NA_KROWS and nctx % 256 == 0

    xs = x[0]
    cs = ctx[0]
    tm_x = min(1024, seq)
    tm_half = min(512, seq)
    tm_merge = min(256, seq)
    tq_mla = min(1024, seq)
    tk_mla = min(512, seq)

    cos_x, sin_x = _rope_tables_T(seq)
    cos_c = jnp.ones((2, ROPE_NFREQ, nctx), F32)
    sin_c = jnp.zeros((2, ROPE_NFREQ, nctx), F32)
    c_cols = jnp.zeros((d, 128), F32).at[:, 0].set(c[0]).at[:, 1].set(c_ctx)

    w_in_all = _layout_w_in(w_in)
    kv_up = w_kv_up.reshape(depth, MLA_KV_RANK, MLA_HEADS, MLA_NOPE + MLA_V)
    wknT = jnp.swapaxes(kv_up[..., :MLA_NOPE].reshape(depth, MLA_KV_RANK, -1), 1, 2).astype(BF16)
    wvT = jnp.swapaxes(kv_up[..., MLA_NOPE:].reshape(depth, MLA_KV_RANK, -1), 1, 2).astype(BF16)
    gains = (na_q_g[..., None], na_k_g[..., None], mla_kv_g[..., None],
             _pad_rows(mla_q_g[..., None], MLA_QPAD), _pad_rows(mla_k_g[..., None], MLA_QPAD))
    bias = _na_bias_tables(na_rel_bias, rows)
    conv_w8 = _pad_rows(conv_w, 8)
    ffn_cw8 = _pad_rows(jnp.concatenate([ffn_conv_w, ffn_conv_b[:, None, :]], axis=1), 8)
    wa, wb, wc, wo, w_up_all, w_down_all = (w.astype(BF16) for w in (w_br_a, w_br_b, w_br_c, w_o, w_up, w_down))
    n1g = norm1_g[:, None, :]
    n2g = norm2_g[:, None, :]

    for l in range(depth):
        last = l == depth - 1
        mod = _modulation(c_cols, w_mod, b_mod, l)

        pgc = _norm_matmul(cs, n1g, mod, 0, 1, w_in_all, l, nctx, IN_TN, "in_proj")
        naqcT, nakc, navcT, mqcT, mkc, mvcT, qnc, knc = _prep(pgc, cos_c, sin_c, gains, wknT, wvT, l, nctx)
        pgx = _norm_matmul(xs, n1g, mod, 0, 0, w_in_all, l, tm_x, IN_TN, "in_proj")
        naqT, nak, navT, mqT, mk, mvT, qn, kn = _prep(pgx, cos_x, sin_x, gains, wknT, wvT, l, tm_half)

        ya = _na_latent(naqT, nak, navT, nakc, navcT, bias, l)
        yb = _mla_attention(mqT, qn, mk, mvT, kn, (mkc, mvcT, knc), tq_mla, tk_mla)
        xs = _merge(ya, yb, pgx, xs, mod, 2, 0, conv_w8, wa, wb, wc, wo, l, tm_merge)
        av = _norm_matmul(xs, n2g, mod, 3, 0, w_up_all, l, tm_x, UP_TN, "ffn_up")
        xs = _ffn_down(av, xs, mod, 5, 0, ffn_cw8, w_down_all, l, tm_merge)

        if not last:
            yac = _na_ctx(naqcT, nakc, navcT)
            ybc = _mla_attention(mqcT, qnc, mkc, mvcT, knc, None, nctx, nctx)
            cs = _merge(yac, ybc, pgc, cs, mod, 2, 1, conv_w8, wa, wb, wc, wo, l, nctx)
            avc = _norm_matmul(cs, n2g, mod, 3, 1, w_up_all, l, nctx, UP_TN, "ffn_up")
            cs = _ffn_down(avc, cs, mod, 5, 1, ffn_cw8, w_down_all, l, nctx)

    return xs[None]
```

```python
import functools

import math

import jax
import jax.numpy as jnp
import numpy as np
from jax import lax
from jax.experimental import pallas as pl
from jax.experimental.pallas import tpu as pltpu

F32 = jnp.float32
BF16 = jnp.bfloat16

D_MODEL = 2048
GRID_W = 64
EPS = 1e-6
NA_HEADS = 8
NA_HEAD_DIM = 64
NA_WIN_H = 8
NA_WIN_W = 16
NA_W = NA_HEADS * NA_HEAD_DIM
MLA_HEADS = 4
MLA_NOPE = 128
MLA_ROPE = 64
MLA_V = 128
MLA_KV_RANK = 256
MLA_QK = MLA_NOPE + MLA_ROPE
MLA_QPAD = 256
ROPE_NFREQ = MLA_ROPE // 4
ROPE_BASE = 10000.0
CONV_CH = 512
D_FF = 5632

R_NA_K = 0
R_NA_V = R_NA_K + NA_W
R_CKV = R_NA_V + NA_W
R_KROPE = R_CKV + MLA_KV_RANK
R_NA_Q = R_KROPE + MLA_ROPE
R_MLA_Q = R_NA_Q + NA_W
R_CONV_U = R_MLA_Q + MLA_HEADS * MLA_QK
R_GATE = R_CONV_U + 3 * CONV_CH

P_NA_K = 0
P_NA_V = 512
P_NA_Q = 1024
P_MLA_Q = 1536
P_CKV = 2560
P_KROPE = 2816
P_QKV_END = 3072
P_UBC = 3072
P_COLS = 4608
G_COLS = 3 * D_MODEL
PG_P = G_COLS

NA_QROWS = 4
NA_KROWS = 12
NA_TQ = NA_QROWS * GRID_W
NA_TK = NA_KROWS * GRID_W
NEG = -1e30
LOG2E = math.log2(math.e)
MLA_BOUND_MARGIN = 1.02
MLA_MAX_SHIFT = 50.0

VMEM_LIMIT = 56 * 1024 * 1024


def _cparams(sem):
    return pltpu.CompilerParams(dimension_semantics=sem, vmem_limit_bytes=VMEM_LIMIT)


def _const_spec(shape):
    nd = len(shape)
    return pl.BlockSpec(shape, lambda *_: (0,) * nd, pipeline_mode=pl.Buffered(1))


def _layer_spec(shape, layer):
    nd = len(shape)
    return pl.BlockSpec((None,) + tuple(shape), lambda *_: (layer,) + (0,) * nd, pipeline_mode=pl.Buffered(1))


MOD_KC = 256


def _mod_kernel(c_ref, w_ref, b_ref, o_ref):
    tn = o_ref.shape[1]
    nk = w_ref.shape[0] // MOD_KC

    def body(kc, acc):
        k0 = pl.multiple_of(kc * MOD_KC, MOD_KC)
        w = w_ref[pl.ds(k0, MOD_KC), :]
        c = c_ref[pl.ds(k0, MOD_KC), :]
        s = c * jax.nn.sigmoid(c)
        out = []
        for r in range(2):
            prod = s[:, r:r + 1] * w
            out.append(acc[r] + jnp.sum(prod.reshape(MOD_KC // 8, 8, tn), axis=0))
        return tuple(out)

    z = jnp.zeros((8, tn), F32)
    acc = lax.fori_loop(0, nk, body, (z, z))
    r0, r1 = [jnp.sum(a, axis=0, keepdims=True) + b_ref[...] for a in acc]
    ridx = lax.broadcasted_iota(jnp.int32, (8, tn), 0)
    o_ref[...] = jnp.where(ridx == 0, r0, jnp.where(ridx == 1, r1, 0.0))


def _modulation(c_cols, w_mod, b_mod, layer):
    depth, d, n = w_mod.shape
    tn = 1536
    return pl.pallas_call(
        _mod_kernel,
        grid=(n // tn,),
        in_specs=[
            _const_spec((d, 128)),
            pl.BlockSpec((None, d, tn), lambda j: (layer, 0, j)),
            pl.BlockSpec((None, 1, tn), lambda j: (layer, 0, j)),
        ],
        out_specs=pl.BlockSpec((8, tn), lambda j: (0, j)),
        out_shape=jax.ShapeDtypeStruct((8, n), F32),
        compiler_params=_cparams(("arbitrary",)),
        name="modulation",
    )(c_cols, w_mod, b_mod.reshape(depth, 1, n))


NORM_RC = 128
IN_TN = 1536
UP_TN = 1024


def _norm_rows(x_ref, g_ref, sh_ref, sc_ref, hx_ref, row):
    tm = x_ref.shape[0]
    g = g_ref[...]
    sc1 = 1.0 + sc_ref[row:row + 1, :]
    sh = sh_ref[row:row + 1, :]

    def body(r, carry):
        r0 = pl.multiple_of(r * NORM_RC, NORM_RC)
        x = x_ref[pl.ds(r0, NORM_RC), :]
        ms = jnp.mean(x * x, axis=-1, keepdims=True)
        y = (x * lax.rsqrt(ms + EPS)) * g
        hx_ref[pl.ds(r0, NORM_RC), :] = (y * sc1 + sh).astype(BF16)
        return carry

    lax.fori_loop(0, tm // NORM_RC, body, 0)


def _norm_matmul_kernel(x_ref, g_ref, sh_ref, sc_ref, w_ref, o_ref, hx_ref, *, row):
    @pl.when(pl.program_id(1) == 0)
    def _():
        _norm_rows(x_ref, g_ref, sh_ref, sc_ref, hx_ref, row)

    o_ref[...] = jnp.dot(hx_ref[...], w_ref[...], preferred_element_type=F32).astype(BF16)


def _norm_matmul(x2d, norm_g, mod, sh_blk, row, w_all, layer, tm, tn, name):
    n, d = x2d.shape
    ncol = w_all.shape[2]
    return pl.pallas_call(
        functools.partial(_norm_matmul_kernel, row=row),
        grid=(n // tm, ncol // tn),
        in_specs=[
            pl.BlockSpec((tm, d), lambda i, j: (i, 0)),
            _layer_spec((1, d), layer),
            pl.BlockSpec((8, d), lambda i, j: (0, sh_blk), pipeline_mode=pl.Buffered(1)),
            pl.BlockSpec((8, d), lambda i, j: (0, sh_blk + 1), pipeline_mode=pl.Buffered(1)),
            pl.BlockSpec((None, d, tn), lambda i, j: (layer, 0, j)),
        ],
        out_specs=pl.BlockSpec((tm, tn), lambda i, j: (i, j)),
        out_shape=jax.ShapeDtypeStruct((n, ncol), BF16),
        scratch_shapes=[pltpu.VMEM((tm, d), BF16)],
        compiler_params=_cparams(("parallel", "arbitrary")),
        name=name,
    )(x2d, norm_g, mod, mod, w_all)


def _rope_rows(r, cos_ref, sin_ref):
    out = []
    for part in range(2):
        cs = cos_ref[part]
        sn = sin_ref[part]
        x1 = r[32 * part:32 * part + 16]
        x2 = r[32 * part + 16:32 * part + 32]
        out.append(x1 * cs - x2 * sn)
        out.append(x2 * cs + x1 * sn)
    return out


def _prep_kernel(p_ref, cos_ref, sin_ref, naqg_ref, nakg_ref, kvg_ref, mqg_ref, mkg_ref,
                 wkn_ref, wv_ref,
                 naqT_ref, nak_ref, navT_ref, mqT_ref, mk_ref, mvT_ref, qn_ref, kn_ref):
    tm = p_ref.shape[0]

    def head_norm_T(cols, g_ref, scale):
        xt = p_ref[:, cols:cols + NA_W].astype(F32).T
        x3 = xt.reshape(NA_HEADS, NA_HEAD_DIM, tm)
        ms = jnp.mean(x3 * x3, axis=1, keepdims=True)
        y = x3 * lax.rsqrt(ms + EPS) * g_ref[...][None]
        if scale != 1.0:
            y = y * scale
        return y.reshape(NA_W, tm)

    naqT_ref[...] = head_norm_T(P_NA_Q, naqg_ref, NA_HEAD_DIM ** -0.5 * LOG2E).astype(BF16)
    nak_ref[...] = head_norm_T(P_NA_K, nakg_ref, 1.0).T.astype(BF16)
    navT_ref[...] = p_ref[:, P_NA_V:P_NA_V + NA_W].astype(F32).T.astype(BF16)

    mqg = mqg_ref[...]
    qscale = MLA_QK ** -0.5 * LOG2E
    for h in range(MLA_HEADS):
        c0 = P_MLA_Q + h * MLA_QPAD
        qt = p_ref[:, c0:c0 + MLA_QPAD].astype(F32).T
        ms = jnp.sum(qt * qt, axis=0, keepdims=True) * (1.0 / MLA_QK)
        y = qt * lax.rsqrt(ms + EPS) * mqg
        r0 = h * MLA_QPAD
        mqT_ref[r0:r0 + MLA_NOPE, :] = (y[:MLA_NOPE] * qscale).astype(BF16)
        parts = _rope_rows(y[MLA_NOPE:MLA_QK], cos_ref, sin_ref)
        for k, part in enumerate(parts):
            a = r0 + MLA_NOPE + 16 * k
            mqT_ref[a:a + 16, :] = (part * qscale).astype(BF16)
        mqT_ref[r0 + MLA_QK:r0 + MLA_QPAD, :] = jnp.zeros((MLA_QPAD - MLA_QK, tm), BF16)
        qnorm = jnp.sqrt(jnp.sum(y * y, axis=0, keepdims=True)) * qscale
        qn_ref[8 * h:8 * h + 8, :] = jnp.broadcast_to(qnorm, (8, tm))

    ct = p_ref[:, P_CKV:P_CKV + MLA_KV_RANK].astype(F32).T
    ms = jnp.mean(ct * ct, axis=0, keepdims=True)
    cn = (ct * lax.rsqrt(ms + EPS) * kvg_ref[...]).astype(BF16)
    mvT_ref[...] = jnp.dot(wv_ref[...], cn, preferred_element_type=F32).astype(BF16)
    knT = jnp.dot(wkn_ref[...], cn, preferred_element_type=F32)
    krT = p_ref[:, P_KROPE:P_KROPE + 128].astype(F32).T[:MLA_ROPE]
    kr_ss = jnp.sum(krT * krT, axis=0, keepdims=True)
    mkg = mkg_ref[...]
    pad = (lax.broadcasted_iota(jnp.int32, (MLA_QPAD - MLA_QK, tm), 0) == 0).astype(F32)
    for h in range(MLA_HEADS):
        kn = knT[h * MLA_NOPE:(h + 1) * MLA_NOPE]
        ms = (jnp.sum(kn * kn, axis=0, keepdims=True) + kr_ss) * (1.0 / MLA_QK)
        rinv = lax.rsqrt(ms + EPS)
        kn = kn * rinv * mkg[:MLA_NOPE]
        kr = krT * rinv * mkg[MLA_NOPE:MLA_QK]
        knorm = jnp.sqrt(jnp.sum(kn * kn, axis=0, keepdims=True) + jnp.sum(kr * kr, axis=0, keepdims=True))
        kn_ref[8 * h:8 * h + 8, :] = jnp.broadcast_to(knorm, (8, tm))
        pieces = [kn] + _rope_rows(kr, cos_ref, sin_ref) + [pad]
        kT = jnp.concatenate(pieces, axis=0)
        mk_ref[:, h * MLA_QPAD:(h + 1) * MLA_QPAD] = kT.T.astype(BF16)


def _prep(pg, cosT, sinT, gains, wknT, wvT, layer, tm):
    n = pg.shape[0]
    naqg, nakg, kvg, mqg, mkg = gains
    return pl.pallas_call(
        _prep_kernel,
        grid=(n // tm,),
        in_specs=[
            pl.BlockSpec((tm, P_QKV_END), lambda i: (i, PG_P // P_QKV_END)),
            pl.BlockSpec((2, ROPE_NFREQ, tm), lambda i: (0, 0, i)),
            pl.BlockSpec((2, ROPE_NFREQ, tm), lambda i: (0, 0, i)),
            _layer_spec((NA_HEAD_DIM, 1), layer),
            _layer_spec((NA_HEAD_DIM, 1), layer),
            _layer_spec((MLA_KV_RANK, 1), layer),
            _layer_spec((MLA_QPAD, 1), layer),
            _layer_spec((MLA_QPAD, 1), layer),
            _layer_spec((MLA_HEADS * MLA_NOPE, MLA_KV_RANK), layer),
            _layer_spec((MLA_HEADS * MLA_V, MLA_KV_RANK), layer),
        ],
        out_specs=[
            pl.BlockSpec((NA_W, tm), lambda i: (0, i)),
            pl.BlockSpec((tm, NA_W), lambda i: (i, 0)),
            pl.BlockSpec((NA_W, tm), lambda i: (0, i)),
            pl.BlockSpec((MLA_HEADS * MLA_QPAD, tm), lambda i: (0, i)),
            pl.BlockSpec((tm, MLA_HEADS * MLA_QPAD), lambda i: (i, 0)),
            pl.BlockSpec((MLA_HEADS * MLA_V, tm), lambda i: (0, i)),
            pl.BlockSpec((MLA_HEADS * 8, tm), lambda i: (0, i)),
            pl.BlockSpec((MLA_HEADS * 8, tm), lambda i: (0, i)),
        ],
        out_shape=[
            jax.ShapeDtypeStruct((NA_W, n), BF16),
            jax.ShapeDtypeStruct((n, NA_W), BF16),
            jax.ShapeDtypeStruct((NA_W, n), BF16),
            jax.ShapeDtypeStruct((MLA_HEADS * MLA_QPAD, n), BF16),
            jax.ShapeDtypeStruct((n, MLA_HEADS * MLA_QPAD), BF16),
            jax.ShapeDtypeStruct((MLA_HEADS * MLA_V, n), BF16),
            jax.ShapeDtypeStruct((MLA_HEADS * 8, n), F32),
            jax.ShapeDtypeStruct((MLA_HEADS * 8, n), F32),
        ],
        compiler_params=_cparams(("parallel",)),
        name="qkv_prep",
    )(pg, cosT, sinT, naqg, nakg, kvg, mqg, mkg, wknT, wvT)


def _na_heads(q_ref, k_refs, vT_refs, bias_ref, o_ref):
    tq = q_ref.shape[1]
    upper = lax.broadcasted_iota(jnp.int32, (128, tq), 0) >= NA_HEAD_DIM
    outs = []
    for h in range(NA_HEADS):
        j, e = divmod(h, 2)
        q2 = q_ref[128 * j:128 * j + 128, :]
        qm = jnp.where(upper == bool(e), q2, jnp.zeros_like(q2))
        s_list = []
        off = 0
        for b, k_ref in enumerate(k_refs):
            s = jnp.dot(k_ref[:, 128 * j:128 * j + 128], qm, preferred_element_type=F32)
            nk = k_ref.shape[0]
            if bias_ref is not None and b < len(k_refs) - 1:
                s = s + bias_ref[0, h, off:off + nk, :]
            off += nk
            s_list.append(s)
        m = s_list[0].max(axis=0, keepdims=True)
        for s in s_list[1:]:
            m = jnp.maximum(m, s.max(axis=0, keepdims=True))
        l = jnp.zeros((1, tq), F32)
        acc = jnp.zeros((NA_HEAD_DIM, tq), F32)
        for s, vT_ref in zip(s_list, vT_refs):
            p = jnp.exp2(s - m)
            l = l + p.sum(axis=0, keepdims=True)
            acc = acc + jnp.dot(vT_ref[NA_HEAD_DIM * h:NA_HEAD_DIM * (h + 1), :], p.astype(BF16),
                                preferred_element_type=F32)
        outs.append(acc / l)
    o_ref[...] = jnp.concatenate(outs, axis=0).T.astype(BF16)


def _na_latent_kernel(q_ref, k0, k1, k2, kc, v0, v1, v2, vc, bias_ref, o_ref):
    _na_heads(q_ref, [k0, k1, k2, kc], [v0, v1, v2, vc], bias_ref, o_ref)


def _na_ctx_kernel(q_ref, kc, vc, o_ref):
    _na_heads(q_ref, [kc], [vc], None, o_ref)


def _na_latent(naqT, nak, navT, nakc, navcT, bias, layer):
    n = naqT.shape[1]
    nb = n // NA_TQ
    nctx = nakc.shape[0]

    def kb(o):
        return lambda i: (jnp.clip(i - 1, 0, nb - 3) + o, 0)

    def vb(o):
        return lambda i: (0, jnp.clip(i - 1, 0, nb - 3) + o)

    def variant(i):
        return (layer, jnp.where(i == 0, 0, jnp.where(i == nb - 1, 2, 1)), 0, 0, 0)

    return pl.pallas_call(
        _na_latent_kernel,
        grid=(nb,),
        in_specs=[
            pl.BlockSpec((NA_W, NA_TQ), lambda i: (0, i)),
            pl.BlockSpec((NA_TQ, NA_W), kb(0)),
            pl.BlockSpec((NA_TQ, NA_W), kb(1)),
            pl.BlockSpec((NA_TQ, NA_W), kb(2)),
            _const_spec((nctx, NA_W)),
            pl.BlockSpec((NA_W, NA_TQ), vb(0)),
            pl.BlockSpec((NA_W, NA_TQ), vb(1)),
            pl.BlockSpec((NA_W, NA_TQ), vb(2)),
            _const_spec((NA_W, nctx)),
            pl.BlockSpec((None, 1, NA_HEADS, NA_TK, NA_TQ), variant),
        ],
        out_specs=pl.BlockSpec((NA_TQ, NA_W), lambda i: (i, 0)),
        out_shape=jax.ShapeDtypeStruct((n, NA_W), BF16),
        compiler_params=_cparams(("parallel",)),
        name="na_latent",
    )(naqT, nak, nak, nak, nakc, navT, navT, navT, navcT, bias)


def _na_ctx(naqcT, nakc, navcT):
    nctx = nakc.shape[0]
    return pl.pallas_call(
        _na_ctx_kernel,
        grid=(1,),
        in_specs=[_const_spec((NA_W, nctx)), _const_spec((nctx, NA_W)), _const_spec((NA_W, nctx))],
        out_specs=pl.BlockSpec((nctx, NA_W), lambda i: (0, 0)),
        out_shape=jax.ShapeDtypeStruct((nctx, NA_W), BF16),
        compiler_params=_cparams(("arbitrary",)),
        name="na_ctx",
    )(naqcT, nakc, navcT)


def _na_bias_tables(rel_bias, rows):
    ndc = 2 * NA_WIN_W - 1
    col = np.arange(GRID_W)
    start_c = np.clip(col - NA_WIN_W // 2, 0, GRID_W - NA_WIN_W)
    in_col = (col[:, None] >= start_c[None, :]) & (col[:, None] < start_c[None, :] + NA_WIN_W)
    dc = col[:, None] - col[None, :] + (NA_WIN_W - 1)
    sel_c = (dc[:, :, None] == np.arange(ndc)).astype(np.float32)
    by_col = jnp.einsum('lhrc,kqc->lhrkq', rel_bias.astype(F32), sel_c, precision=lax.Precision.HIGHEST)
    by_col = jnp.where(in_col, by_col * LOG2E, NEG)
    masked = jnp.full(by_col.shape[:2] + (GRID_W, GRID_W), NEG, F32)
    tabs = []
    for r0, base in ((0, 0), (NA_QROWS, 0), (rows - NA_QROWS, rows - NA_KROWS)):
        key_rows = []
        for a in range(NA_KROWS):
            blocks = []
            for b in range(NA_QROWS):
                qr, kr = r0 + b, base + a
                start_r = min(max(qr - NA_WIN_H // 2, 0), rows - NA_WIN_H)
                inside = start_r <= kr < start_r + NA_WIN_H
                blocks.append(by_col[:, :, kr - qr + NA_WIN_H - 1] if inside else masked)
            key_rows.append(jnp.concatenate(blocks, axis=-1))
        tabs.append(jnp.concatenate(key_rows, axis=-2))
    return jnp.stack(tabs, axis=1)


def _mla_kernel(*refs, tk, has_ctx):
    if has_ctx:
        qT_ref, qn_ref, k_ref, vT_ref, kn_ref, kc_ref, vcT_ref, knc_ref, o_ref = refs[:9]
    else:
        qT_ref, qn_ref, k_ref, vT_ref, kn_ref, o_ref = refs[:6]
    m_ref, l_ref, acc_ref, al_ref, s_a, s_b, p_a, p_b, qs_ref = refs[-9:]
    tq = qT_ref.shape[1]
    nk = k_ref.shape[0] // tk

    kmax = jnp.max(kn_ref[0:1, :])
    if has_ctx:
        kmax = jnp.maximum(kmax, jnp.max(knc_ref[0:1, :]))
    bound = qn_ref[0:1, :] * (kmax * MLA_BOUND_MARGIN)
    shift_ok = jnp.max(bound) <= MLA_MAX_SHIFT

    @pl.when(shift_ok)
    def _():
        qs_ref[...] = qT_ref[...]
        row0 = lax.broadcasted_iota(jnp.int32, (16, tq), 0) == 0
        qs_ref[MLA_QK:MLA_QK + 16, :] = jnp.where(row0, -bound, 0.0).astype(BF16)

        def chunk(k_blk, vT_blk):
            p = jnp.exp2(jnp.dot(k_blk, qs_ref[...], preferred_element_type=F32))
            return p.sum(axis=0, keepdims=True), jnp.dot(vT_blk, p.astype(BF16), preferred_element_type=F32)

        if has_ctx:
            l0, a0 = chunk(kc_ref[...], vcT_ref[...])
            l_ref[...] = l0
            acc_ref[...] = a0
        else:
            l_ref[...] = jnp.zeros((1, tq), F32)
            acc_ref[...] = jnp.zeros((MLA_V, tq), F32)
        unroll = next(u for u in (8, 4, 2, 1) if nk % u == 0)

        def body(it, carry):
            l_sum = l_ref[...]
            a_sum = acc_ref[...]
            for u in range(unroll):
                k0 = pl.multiple_of((it * unroll + u) * tk, tk)
                lp, ap = chunk(k_ref[pl.ds(k0, tk), :], vT_ref[:, pl.ds(k0, tk)])
                l_sum = l_sum + lp
                a_sum = a_sum + ap
            l_ref[...] = l_sum
            acc_ref[...] = a_sum
            return carry

        lax.fori_loop(0, nk // unroll, body, 0)

    @pl.when(jnp.logical_not(shift_ok))
    def _():
        if has_ctx:
            s = jnp.dot(kc_ref[...], qT_ref[...], preferred_element_type=F32)
            m = s.max(axis=0, keepdims=True)
            p = jnp.exp2(s - m)
            m_ref[...] = m
            l_ref[...] = p.sum(axis=0, keepdims=True)
            acc_ref[...] = jnp.dot(vcT_ref[...], p.astype(BF16), preferred_element_type=F32)
        else:
            m_ref[...] = jnp.full((1, tq), NEG, F32)
            l_ref[...] = jnp.zeros((1, tq), F32)
            acc_ref[...] = jnp.zeros((MLA_V, tq), F32)

        def scores(chunk, s_ref):
            k0 = pl.multiple_of(chunk * tk, tk)
            s_ref[...] = jnp.dot(k_ref[pl.ds(k0, tk), :], qT_ref[...], preferred_element_type=F32)

        def softmax(s_ref, p_ref):
            s = s_ref[...]
            m_old = m_ref[...]
            m_new = jnp.maximum(m_old, s.max(axis=0, keepdims=True))
            alpha = jnp.exp2(m_old - m_new)
            p = jnp.exp2(s - m_new)
            l_ref[...] = alpha * l_ref[...] + p.sum(axis=0, keepdims=True)
            m_ref[...] = m_new
            p_ref[...] = p.astype(BF16)
            return alpha

        def values(chunk, p_ref, alpha):
            k0 = pl.multiple_of(chunk * tk, tk)
            acc_ref[...] = alpha * acc_ref[...] + jnp.dot(vT_ref[:, pl.ds(k0, tk)], p_ref[...],
                                                          preferred_element_type=F32)

        if nk % 2 == 0:
            scores(0, s_a)
            p_b[...] = jnp.zeros(p_b.shape, BF16)
            al_ref[...] = jnp.ones((1, tq), F32)

            def body(it, carry):
                c0 = 2 * it
                scores(c0 + 1, s_b)
                a0 = softmax(s_a, p_a)
                values(jnp.maximum(c0 - 1, 0), p_b, al_ref[...])
                scores(jnp.minimum(c0 + 2, nk - 1), s_a)
                a1 = softmax(s_b, p_b)
                values(c0, p_a, a0)
                al_ref[...] = a1
                return carry

            lax.fori_loop(0, nk // 2, body, 0)
            values(nk - 1, p_b, al_ref[...])
        else:
            for c in range(nk):
                scores(c, s_a)
                values(c, p_a, softmax(s_a, p_a))

    o_ref[...] = (acc_ref[...] / l_ref[...]).T.astype(BF16)


def _mla_attention(mqT, qn, mk, mvT, kn, ctx_kv, tq, tk):
    nq = mqT.shape[1]
    nkeys = mk.shape[0]
    has_ctx = ctx_kv is not None
    in_specs = [
        pl.BlockSpec((MLA_QPAD, tq), lambda h, i: (h, i)),
        pl.BlockSpec((8, tq), lambda h, i: (h, i)),
        pl.BlockSpec((nkeys, MLA_QPAD), lambda h, i: (0, h)),
        pl.BlockSpec((MLA_V, nkeys), lambda h, i: (h, 0)),
        pl.BlockSpec((8, nkeys), lambda h, i: (h, 0)),
    ]
    args = [mqT, qn, mk, mvT, kn]
    if has_ctx:
        kc, vcT, knc = ctx_kv
        nctx = kc.shape[0]
        in_specs += [
            pl.BlockSpec((nctx, MLA_QPAD), lambda h, i: (0, h)),
            pl.BlockSpec((MLA_V, nctx), lambda h, i: (h, 0)),
            pl.BlockSpec((8, nctx), lambda h, i: (h, 0)),
        ]
        args += [kc, vcT, knc]
    return pl.pallas_call(
        functools.partial(_mla_kernel, tk=tk, has_ctx=has_ctx),
        grid=(MLA_HEADS, nq // tq),
        in_specs=in_specs,
        out_specs=pl.BlockSpec((tq, MLA_V), lambda h, i: (i, h)),
        out_shape=jax.ShapeDtypeStruct((nq, MLA_HEADS * MLA_V), BF16),
        scratch_shapes=[pltpu.VMEM((1, tq), F32), pltpu.VMEM((1, tq), F32), pltpu.VMEM((MLA_V, tq), F32),
                        pltpu.VMEM((1, tq), F32),
                        pltpu.VMEM((tk, tq), F32), pltpu.VMEM((tk, tq), F32),
                        pltpu.VMEM((tk, tq), BF16), pltpu.VMEM((tk, tq), BF16),
                        pltpu.VMEM((MLA_QPAD, tq), BF16)],
        compiler_params=_cparams(("parallel", "arbitrary")),
        name="mla_attention",
    )(*args)


def _conv3_rows(x, prev_row, next_row, w):
    tm = x.shape[0]
    rows = lax.broadcasted_iota(jnp.int32, (tm, 1), 0)
    xm1 = jnp.where(rows == 0, prev_row, pltpu.roll(x, 1, 0))
    xp1 = jnp.where(rows == tm - 1, next_row, pltpu.roll(x, tm - 1, 0))
    return w[0:1, :] * xm1 + w[1:2, :] * x + w[2:3, :] * xp1


HALO = 16


def _merge_kernel(ya_ref, yb_ref, ubc_ref, prev_ref, next_ref, gate_ref, x_ref, g1_ref, cw_ref,
                  wa_ref, wb_ref, wc_ref, wo_ref, o_ref, *, row):
    i = pl.program_id(0)
    nblk = pl.num_programs(0)
    ubc = ubc_ref[...].astype(F32)
    u = ubc[:, :CONV_CH]
    gb = ubc[:, CONV_CH:2 * CONV_CH]
    gc = ubc[:, 2 * CONV_CH:]
    cu = gc * u
    pr = prev_ref[...].astype(F32)[HALO - 1:HALO, :]
    nx = next_ref[...].astype(F32)[0:1, :]
    cu_prev = pr[:, 2 * CONV_CH:] * pr[:, :CONV_CH] * (i > 0).astype(F32)
    cu_next = nx[:, 2 * CONV_CH:] * nx[:, :CONV_CH] * (i < nblk - 1).astype(F32)
    yc = (gb * _conv3_rows(cu, cu_prev, cu_next, cw_ref[...])).astype(BF16)

    a = jnp.dot(ya_ref[...], wa_ref[...], preferred_element_type=F32)
    b = jnp.dot(yb_ref[...], wb_ref[...], preferred_element_type=F32)
    c = jnp.dot(yc, wc_ref[...], preferred_element_type=F32)
    d = D_MODEL
    merged = (jax.nn.sigmoid(gate_ref[:, :d].astype(F32)) * a
              + jax.nn.sigmoid(gate_ref[:, d:2 * d].astype(F32)) * b
              + jax.nn.sigmoid(gate_ref[:, 2 * d:].astype(F32)) * c)
    mix = jnp.dot(merged.astype(BF16), wo_ref[...], preferred_element_type=F32)
    o_ref[...] = x_ref[...] + g1_ref[row:row + 1, :] * mix


def _merge(ya, yb, pg, x2d, mod, g1_blk, row, conv_w8, wa, wb, wc, wo, layer, tm):
    n, d = x2d.shape
    ubc_blk = (PG_P + P_UBC) // (3 * CONV_CH)
    hb = tm // HALO
    last = n // HALO - 1
    return pl.pallas_call(
        functools.partial(_merge_kernel, row=row),
        grid=(n // tm,),
        in_specs=[
            pl.BlockSpec((tm, NA_W), lambda i: (i, 0)),
            pl.BlockSpec((tm, MLA_HEADS * MLA_V), lambda i: (i, 0)),
            pl.BlockSpec((tm, 3 * CONV_CH), lambda i: (i, ubc_blk)),
            pl.BlockSpec((HALO, 3 * CONV_CH), lambda i: (jnp.maximum(i * hb - 1, 0), ubc_blk)),
            pl.BlockSpec((HALO, 3 * CONV_CH), lambda i: (jnp.minimum((i + 1) * hb, last), ubc_blk)),
            pl.BlockSpec((tm, G_COLS), lambda i: (i, 0)),
            pl.BlockSpec((tm, d), lambda i: (i, 0)),
            pl.BlockSpec((8, d), lambda i: (0, g1_blk), pipeline_mode=pl.Buffered(1)),
            _layer_spec((8, CONV_CH), layer),
            _layer_spec(wa.shape[1:], layer),
            _layer_spec(wb.shape[1:], layer),
            _layer_spec(wc.shape[1:], layer),
            _layer_spec(wo.shape[1:], layer),
        ],
        out_specs=pl.BlockSpec((tm, d), lambda i: (i, 0)),
        out_shape=jax.ShapeDtypeStruct((n, d), F32),
        compiler_params=_cparams(("parallel",)),
        name="merge_out",
    )(ya, yb, pg, pg, pg, pg, x2d, mod, conv_w8, wa, wb, wc, wo)


FFN_KC = 512


def _ffn_down_kernel(a_ref, prev_ref, next_ref, val_ref, cw_ref, wd_ref, x_ref, g2_ref, o_ref,
                     acc_ref, h_a, h_b, *, row):
    i = pl.program_id(0)
    nblk = pl.num_programs(0)
    has_prev = (i > 0).astype(F32)
    has_next = (i < nblk - 1).astype(F32)
    nchunks = D_FF // FFN_KC

    def hidden(c, h_ref):
        sl = slice(c * FFN_KC, (c + 1) * FFN_KC)
        w = cw_ref[:, sl]
        a = a_ref[:, sl].astype(F32)
        pr = prev_ref[:, sl].astype(F32)[HALO - 1:HALO, :] * has_prev
        nx = next_ref[:, sl].astype(F32)[0:1, :] * has_next
        a = _conv3_rows(a, pr, nx, w) + w[3:4, :]
        h_ref[...] = (a * jax.nn.sigmoid(a) * val_ref[:, sl].astype(F32)).astype(BF16)

    bufs = (h_a, h_b)
    hidden(0, h_a)
    for c in range(nchunks):
        if c + 1 < nchunks:
            hidden(c + 1, bufs[(c + 1) % 2])
        part = jnp.dot(bufs[c % 2][...], wd_ref[c * FFN_KC:(c + 1) * FFN_KC, :], preferred_element_type=F32)
        if c == 0:
            acc_ref[...] = part
        else:
            acc_ref[...] += part
    o_ref[...] = x_ref[...] + g2_ref[row:row + 1, :] * acc_ref[...]


def _ffn_down(av, x2d, mod, g2_blk, row, cw8, wd, layer, tm):
    n, d = x2d.shape
    hb = tm // HALO
    last = n // HALO - 1
    return pl.pallas_call(
        functools.partial(_ffn_down_kernel, row=row),
        grid=(n // tm,),
        in_specs=[
            pl.BlockSpec((tm, D_FF), lambda i: (i, 0)),
            pl.BlockSpec((HALO, D_FF), lambda i: (jnp.maximum(i * hb - 1, 0), 0)),
            pl.BlockSpec((HALO, D_FF), lambda i: (jnp.minimum((i + 1) * hb, last), 0)),
            pl.BlockSpec((tm, D_FF), lambda i: (i, 1)),
            _layer_spec((8, D_FF), layer),
            _layer_spec((D_FF, d), layer),
            pl.BlockSpec((tm, d), lambda i: (i, 0)),
            pl.BlockSpec((8, d), lambda i: (0, g2_blk), pipeline_mode=pl.Buffered(1)),
        ],
        out_specs=pl.BlockSpec((tm, d), lambda i: (i, 0)),
        out_shape=jax.ShapeDtypeStruct((n, d), F32),
        scratch_shapes=[pltpu.VMEM((tm, d), F32), pltpu.VMEM((tm, FFN_KC), BF16), pltpu.VMEM((tm, FFN_KC), BF16)],
        compiler_params=_cparams(("parallel",)),
        name="ffn_down",
    )(av, av, av, av, cw8, wd, x2d, mod)


def _layout_w_in(w):
    z64 = jnp.zeros(w.shape[:-1] + (64,), w.dtype)
    cols = [w[..., R_GATE:], w[..., R_NA_K:R_NA_V], w[..., R_NA_V:R_CKV], w[..., R_NA_Q:R_MLA_Q]]
    for h in range(MLA_HEADS):
        cols += [w[..., R_MLA_Q + h * MLA_QK:R_MLA_Q + (h + 1) * MLA_QK], z64]
    cols += [w[..., R_CKV:R_KROPE], w[..., R_KROPE:R_NA_Q], z64, z64, z64, w[..., R_CONV_U:R_GATE]]
    return jnp.concatenate(cols, axis=-1).astype(BF16)


def _pad_rows(a, rows):
    return jnp.concatenate([a, jnp.zeros((a.shape[0], rows - a.shape[1], a.shape[2]), a.dtype)], axis=1)


def _rope_tables_T(n_tok):
    t = jnp.arange(n_tok)
    row = (t // GRID_W).astype(F32)
    col = (t % GRID_W).astype(F32)
    inv = ROPE_BASE ** (-jnp.arange(ROPE_NFREQ, dtype=F32) / ROPE_NFREQ)
    ang = jnp.stack([inv[:, None] * row[None, :], inv[:, None] * col[None, :]], axis=0)
    return jnp.cos(ang), jnp.sin(ang)


def kernel(x, c, ctx, c_ctx, w_mod, b_mod, norm1_g, w_in, na_q_g, na_k_g, na_rel_bias, mla_kv_g, w_kv_up,
           mla_q_g, mla_k_g, conv_w, w_br_a, w_br_b, w_br_c, w_o, norm2_g, w_up, ffn_conv_w, ffn_conv_b, w_down):
    bsz, seq, d = x.shape
    nctx = ctx.shape[1]
    depth = w_mod.shape[0]
    rows = seq // GRID_W
    assert bsz == 1 and d == D_MODEL and seq % NA_TQ == 0 and rows >= NA_KROWS and nctx % 256 == 0

    xs = x[0]
    cs = ctx[0]
    tm_x = min(1024, seq)
    tm_half = min(512, seq)
    tm_merge = min(256, seq)
    tq_mla = min(1024, seq)
    tk_mla = min(512, seq)

    cos_x, sin_x = _rope_tables_T(seq)
    cos_c = jnp.ones((2, ROPE_NFREQ, nctx), F32)
    sin_c = jnp.zeros((2, ROPE_NFREQ, nctx), F32)
    c_cols = jnp.zeros((d, 128), F32).at[:, 0].set(c[0]).at[:, 1].set(c_ctx)

    w_in_l = [_layout_w_in(w_in[l:l + 1]) for l in range(depth)]
    kv_up = w_kv_up.reshape(depth, MLA_KV_RANK, MLA_HEADS, MLA_NOPE + MLA_V)
    wknT = jnp.swapaxes(kv_up[..., :MLA_NOPE].reshape(depth, MLA_KV_RANK, -1), 1, 2).astype(BF16)
    wvT = jnp.swapaxes(kv_up[..., MLA_NOPE:].reshape(depth, MLA_KV_RANK, -1), 1, 2).astype(BF16)
    gains = (na_q_g[..., None], na_k_g[..., None], mla_kv_g[..., None],
             _pad_rows(mla_q_g[..., None], MLA_QPAD), _pad_rows(mla_k_g[..., None], MLA_QPAD))
    bias = _na_bias_tables(na_rel_bias, rows)
    conv_w8 = _pad_rows(conv_w, 8)
    ffn_cw8 = _pad_rows(jnp.concatenate([ffn_conv_w, ffn_conv_b[:, None, :]], axis=1), 8)
    wa, wb, wc, wo, w_up_all, w_down_all = (w.astype(BF16) for w in (w_br_a, w_br_b, w_br_c, w_o, w_up, w_down))
    n1g = norm1_g[:, None, :]
    n2g = norm2_g[:, None, :]

    for l in range(depth):
        last = l == depth - 1
        mod = _modulation(c_cols, w_mod, b_mod, l)

        pgc = _norm_matmul(cs, n1g[l:l + 1], mod, 0, 1, w_in_l[l], 0, nctx, IN_TN, "in_proj")
        naqcT, nakc, navcT, mqcT, mkc, mvcT, qnc, knc = _prep(pgc, cos_c, sin_c, gains, wknT, wvT, l, nctx)
        pgx = _norm_matmul(xs, n1g[l:l + 1], mod, 0, 0, w_in_l[l], 0, tm_x, IN_TN, "in_proj")
        naqT, nak, navT, mqT, mk, mvT, qn, kn = _prep(pgx, cos_x, sin_x, gains, wknT, wvT, l, tm_half)

        ya = _na_latent(naqT, nak, navT, nakc, navcT, bias, l)
        yb = _mla_attention(mqT, qn, mk, mvT, kn, (mkc, mvcT, knc), tq_mla, tk_mla)
        xs = _merge(ya, yb, pgx, xs, mod, 2, 0, conv_w8, wa, wb, wc, wo, l, tm_merge)
        av = _norm_matmul(xs, n2g, mod, 3, 0, w_up_all, l, tm_x, UP_TN, "ffn_up")
        xs = _ffn_down(av, xs, mod, 5, 0, ffn_cw8, w_down_all, l, tm_merge)

        if not last:
            yac = _na_ctx(naqcT, nakc, navcT)
            ybc = _mla_attention(mqcT, qnc, mkc, mvcT, knc, None, nctx, nctx)
            cs = _merge(yac, ybc, pgc, cs, mod, 2, 1, conv_w8, wa, wb, wc, wo, l, nctx)
            avc = _norm_matmul(cs, n2g, mod, 3, 1, w_up_all, l, nctx, UP_TN, "ffn_up")
            cs = _ffn_down(avc, cs, mod, 5, 1, ffn_cw8, w_down_all, l, nctx)

    return xs[None]
```

```python
import functools

import math

import jax
import jax.numpy as jnp
import numpy as np
from jax import lax
from jax.experimental import pallas as pl
from jax.experimental.pallas import tpu as pltpu

F32 = jnp.float32
BF16 = jnp.bfloat16

D_MODEL = 2048
GRID_W = 64
EPS = 1e-6
NA_HEADS = 8
NA_HEAD_DIM = 64
NA_WIN_H = 8
NA_WIN_W = 16
NA_W = NA_HEADS * NA_HEAD_DIM
MLA_HEADS = 4
MLA_NOPE = 128
MLA_ROPE = 64
MLA_V = 128
MLA_KV_RANK = 256
MLA_QK = MLA_NOPE + MLA_ROPE
MLA_QPAD = 256
ROPE_NFREQ = MLA_ROPE // 4
ROPE_BASE = 10000.0
CONV_CH = 512
D_FF = 5632

R_NA_K = 0
R_NA_V = R_NA_K + NA_W
R_CKV = R_NA_V + NA_W
R_KROPE = R_CKV + MLA_KV_RANK
R_NA_Q = R_KROPE + MLA_ROPE
R_MLA_Q = R_NA_Q + NA_W
R_CONV_U = R_MLA_Q + MLA_HEADS * MLA_QK
R_GATE = R_CONV_U + 3 * CONV_CH

P_NA_K = 0
P_NA_V = 512
P_NA_Q = 1024
P_MLA_Q = 1536
P_CKV = 2560
P_KROPE = 2816
P_QKV_END = 3072
P_UBC = 3072
P_COLS = 4608
G_COLS = 3 * D_MODEL
PG_P = G_COLS

NA_QROWS = 4
NA_KROWS = 12
NA_TQ = NA_QROWS * GRID_W
NA_TK = NA_KROWS * GRID_W
NEG = -1e30
LOG2E = math.log2(math.e)
MLA_BOUND_MARGIN = 1.02
MLA_MAX_SHIFT = 50.0

VMEM_LIMIT = 56 * 1024 * 1024


def _cparams(sem):
    return pltpu.CompilerParams(dimension_semantics=sem, vmem_limit_bytes=VMEM_LIMIT)


def _const_spec(shape):
    nd = len(shape)
    return pl.BlockSpec(shape, lambda *_: (0,) * nd, pipeline_mode=pl.Buffered(1))


def _layer_spec(shape, layer):
    nd = len(shape)
    return pl.BlockSpec((None,) + tuple(shape), lambda *_: (layer,) + (0,) * nd, pipeline_mode=pl.Buffered(1))


MOD_KC = 256


def _mod_kernel(c_ref, w_ref, b_ref, o_ref):
    tn = o_ref.shape[1]
    nk = w_ref.shape[0] // MOD_KC

    def body(kc, acc):
        k0 = pl.multiple_of(kc * MOD_KC, MOD_KC)
        w = w_ref[pl.ds(k0, MOD_KC), :]
        c = c_ref[pl.ds(k0, MOD_KC), :]
        s = c * jax.nn.sigmoid(c)
        out = []
        for r in range(2):
            prod = s[:, r:r + 1] * w
            out.append(acc[r] + jnp.sum(prod.reshape(MOD_KC // 8, 8, tn), axis=0))
        return tuple(out)

    z = jnp.zeros((8, tn), F32)
    acc = lax.fori_loop(0, nk, body, (z, z))
    r0, r1 = [jnp.sum(a, axis=0, keepdims=True) + b_ref[...] for a in acc]
    ridx = lax.broadcasted_iota(jnp.int32, (8, tn), 0)
    o_ref[...] = jnp.where(ridx == 0, r0, jnp.where(ridx == 1, r1, 0.0))


def _modulation(c_cols, w_mod, b_mod, layer):
    depth, d, n = w_mod.shape
    tn = 1536
    return pl.pallas_call(
        _mod_kernel,
        grid=(n // tn,),
        in_specs=[
            _const_spec((d, 128)),
            pl.BlockSpec((None, d, tn), lambda j: (layer, 0, j)),
            pl.BlockSpec((None, 1, tn), lambda j: (layer, 0, j)),
        ],
        out_specs=pl.BlockSpec((8, tn), lambda j: (0, j)),
        out_shape=jax.ShapeDtypeStruct((8, n), F32),
        compiler_params=_cparams(("arbitrary",)),
        name="modulation",
    )(c_cols, w_mod, b_mod.reshape(depth, 1, n))


NORM_RC = 128
IN_TN = 1536
UP_TN = 1024


def _norm_rows(x_ref, g_ref, sh_ref, sc_ref, hx_ref, row):
    tm = x_ref.shape[0]
    g = g_ref[...]
    sc1 = 1.0 + sc_ref[row:row + 1, :]
    sh = sh_ref[row:row + 1, :]

    def body(r, carry):
        r0 = pl.multiple_of(r * NORM_RC, NORM_RC)
        x = x_ref[pl.ds(r0, NORM_RC), :]
        ms = jnp.mean(x * x, axis=-1, keepdims=True)
        y = (x * lax.rsqrt(ms + EPS)) * g
        hx_ref[pl.ds(r0, NORM_RC), :] = (y * sc1 + sh).astype(BF16)
        return carry

    lax.fori_loop(0, tm // NORM_RC, body, 0)


def _norm_matmul_kernel(x_ref, g_ref, sh_ref, sc_ref, w_ref, o_ref, hx_ref, *, row):
    @pl.when(pl.program_id(1) == 0)
    def _():
        _norm_rows(x_ref, g_ref, sh_ref, sc_ref, hx_ref, row)

    o_ref[...] = jnp.dot(hx_ref[...], w_ref[...], preferred_element_type=F32).astype(BF16)


def _norm_matmul(x2d, norm_g, mod, sh_blk, row, w_all, layer, tm, tn, name):
    n, d = x2d.shape
    ncol = w_all.shape[2]
    return pl.pallas_call(
        functools.partial(_norm_matmul_kernel, row=row),
        grid=(n // tm, ncol // tn),
        in_specs=[
            pl.BlockSpec((tm, d), lambda i, j: (i, 0)),
            _layer_spec((1, d), layer),
            pl.BlockSpec((8, d), lambda i, j: (0, sh_blk), pipeline_mode=pl.Buffered(1)),
            pl.BlockSpec((8, d), lambda i, j: (0, sh_blk + 1), pipeline_mode=pl.Buffered(1)),
            pl.BlockSpec((None, d, tn), lambda i, j: (layer, 0, j)),
        ],
        out_specs=pl.BlockSpec((tm, tn), lambda i, j: (i, j)),
        out_shape=jax.ShapeDtypeStruct((n, ncol), BF16),
        scratch_shapes=[pltpu.VMEM((tm, d), BF16)],
        compiler_params=_cparams(("parallel", "arbitrary")),
        name=name,
    )(x2d, norm_g, mod, mod, w_all)


def _rope_rows(r, cos_ref, sin_ref):
    out = []
    for part in range(2):
        cs = cos_ref[part]
        sn = sin_ref[part]
        x1 = r[32 * part:32 * part + 16]
        x2 = r[32 * part + 16:32 * part + 32]
        out.append(x1 * cs - x2 * sn)
        out.append(x2 * cs + x1 * sn)
    return out


def _prep_kernel(p_ref, cos_ref, sin_ref, naqg_ref, nakg_ref, kvg_ref, mqg_ref, mkg_ref,
                 wkn_ref, wv_ref,
                 naqT_ref, nak_ref, navT_ref, mqT_ref, mk_ref, mvT_ref, qn_ref, kn_ref):
    tm = p_ref.shape[0]

    def head_norm_T(cols, g_ref, scale):
        xt = p_ref[:, cols:cols + NA_W].astype(F32).T
        x3 = xt.reshape(NA_HEADS, NA_HEAD_DIM, tm)
        ms = jnp.mean(x3 * x3, axis=1, keepdims=True)
        y = x3 * lax.rsqrt(ms + EPS) * g_ref[...][None]
        if scale != 1.0:
            y = y * scale
        return y.reshape(NA_W, tm)

    naqT_ref[...] = head_norm_T(P_NA_Q, naqg_ref, NA_HEAD_DIM ** -0.5 * LOG2E).astype(BF16)
    nak_ref[...] = head_norm_T(P_NA_K, nakg_ref, 1.0).T.astype(BF16)
    navT_ref[...] = p_ref[:, P_NA_V:P_NA_V + NA_W].astype(F32).T.astype(BF16)

    mqg = mqg_ref[...]
    qscale = MLA_QK ** -0.5 * LOG2E
    for h in range(MLA_HEADS):
        c0 = P_MLA_Q + h * MLA_QPAD
        qt = p_ref[:, c0:c0 + MLA_QPAD].astype(F32).T
        ms = jnp.sum(qt * qt, axis=0, keepdims=True) * (1.0 / MLA_QK)
        y = qt * lax.rsqrt(ms + EPS) * mqg
        r0 = h * MLA_QPAD
        mqT_ref[r0:r0 + MLA_NOPE, :] = (y[:MLA_NOPE] * qscale).astype(BF16)
        parts = _rope_rows(y[MLA_NOPE:MLA_QK], cos_ref, sin_ref)
        for k, part in enumerate(parts):
            a = r0 + MLA_NOPE + 16 * k
            mqT_ref[a:a + 16, :] = (part * qscale).astype(BF16)
        mqT_ref[r0 + MLA_QK:r0 + MLA_QPAD, :] = jnp.zeros((MLA_QPAD - MLA_QK, tm), BF16)
        qnorm = jnp.sqrt(jnp.sum(y * y, axis=0, keepdims=True)) * qscale
        qn_ref[8 * h:8 * h + 8, :] = jnp.broadcast_to(qnorm, (8, tm))

    ct = p_ref[:, P_CKV:P_CKV + MLA_KV_RANK].astype(F32).T
    ms = jnp.mean(ct * ct, axis=0, keepdims=True)
    cn = (ct * lax.rsqrt(ms + EPS) * kvg_ref[...]).astype(BF16)
    mvT_ref[...] = jnp.dot(wv_ref[...], cn, preferred_element_type=F32).astype(BF16)
    knT = jnp.dot(wkn_ref[...], cn, preferred_element_type=F32)
    krT = p_ref[:, P_KROPE:P_KROPE + 128].astype(F32).T[:MLA_ROPE]
    kr_ss = jnp.sum(krT * krT, axis=0, keepdims=True)
    mkg = mkg_ref[...]
    pad = (lax.broadcasted_iota(jnp.int32, (MLA_QPAD - MLA_QK, tm), 0) == 0).astype(F32)
    for h in range(MLA_HEADS):
        kn = knT[h * MLA_NOPE:(h + 1) * MLA_NOPE]
        ms = (jnp.sum(kn * kn, axis=0, keepdims=True) + kr_ss) * (1.0 / MLA_QK)
        rinv = lax.rsqrt(ms + EPS)
        kn = kn * rinv * mkg[:MLA_NOPE]
        kr = krT * rinv * mkg[MLA_NOPE:MLA_QK]
        knorm = jnp.sqrt(jnp.sum(kn * kn, axis=0, keepdims=True) + jnp.sum(kr * kr, axis=0, keepdims=True))
        kn_ref[8 * h:8 * h + 8, :] = jnp.broadcast_to(knorm, (8, tm))
        pieces = [kn] + _rope_rows(kr, cos_ref, sin_ref) + [pad]
        kT = jnp.concatenate(pieces, axis=0)
        mk_ref[:, h * MLA_QPAD:(h + 1) * MLA_QPAD] = kT.T.astype(BF16)


def _prep(pg, cosT, sinT, gains, wknT, wvT, layer, tm):
    n = pg.shape[0]
    naqg, nakg, kvg, mqg, mkg = gains
    return pl.pallas_call(
        _prep_kernel,
        grid=(n // tm,),
        in_specs=[
            pl.BlockSpec((tm, P_QKV_END), lambda i: (i, PG_P // P_QKV_END)),
            pl.BlockSpec((2, ROPE_NFREQ, tm), lambda i: (0, 0, i)),
            pl.BlockSpec((2, ROPE_NFREQ, tm), lambda i: (0, 0, i)),
            _layer_spec((NA_HEAD_DIM, 1), layer),
            _layer_spec((NA_HEAD_DIM, 1), layer),
            _layer_spec((MLA_KV_RANK, 1), layer),
            _layer_spec((MLA_QPAD, 1), layer),
            _layer_spec((MLA_QPAD, 1), layer),
            _layer_spec((MLA_HEADS * MLA_NOPE, MLA_KV_RANK), layer),
            _layer_spec((MLA_HEADS * MLA_V, MLA_KV_RANK), layer),
        ],
        out_specs=[
            pl.BlockSpec((NA_W, tm), lambda i: (0, i)),
            pl.BlockSpec((tm, NA_W), lambda i: (i, 0)),
            pl.BlockSpec((NA_W, tm), lambda i: (0, i)),
            pl.BlockSpec((MLA_HEADS * MLA_QPAD, tm), lambda i: (0, i)),
            pl.BlockSpec((tm, MLA_HEADS * MLA_QPAD), lambda i: (i, 0)),
            pl.BlockSpec((MLA_HEADS * MLA_V, tm), lambda i: (0, i)),
            pl.BlockSpec((MLA_HEADS * 8, tm), lambda i: (0, i)),
            pl.BlockSpec((MLA_HEADS * 8, tm), lambda i: (0, i)),
        ],
        out_shape=[
            jax.ShapeDtypeStruct((NA_W, n), BF16),
            jax.ShapeDtypeStruct((n, NA_W), BF16),
            jax.ShapeDtypeStruct((NA_W, n), BF16),
            jax.ShapeDtypeStruct((MLA_HEADS * MLA_QPAD, n), BF16),
            jax.ShapeDtypeStruct((n, MLA_HEADS * MLA_QPAD), BF16),
            jax.ShapeDtypeStruct((MLA_HEADS * MLA_V, n), BF16),
            jax.ShapeDtypeStruct((MLA_HEADS * 8, n), F32),
            jax.ShapeDtypeStruct((MLA_HEADS * 8, n), F32),
        ],
        compiler_params=_cparams(("parallel",)),
        name="qkv_prep",
    )(pg, cosT, sinT, naqg, nakg, kvg, mqg, mkg, wknT, wvT)


def _na_heads(q_ref, k_refs, vT_refs, bias_ref, o_ref):
    tq = q_ref.shape[1]
    upper = lax.broadcasted_iota(jnp.int32, (128, tq), 0) >= NA_HEAD_DIM
    outs = []
    for h in range(NA_HEADS):
        j, e = divmod(h, 2)
        q2 = q_ref[128 * j:128 * j + 128, :]
        qm = jnp.where(upper == bool(e), q2, jnp.zeros_like(q2))
        s_list = []
        off = 0
        for b, k_ref in enumerate(k_refs):
            s = jnp.dot(k_ref[:, 128 * j:128 * j + 128], qm, preferred_element_type=F32)
            nk = k_ref.shape[0]
            if bias_ref is not None and b < len(k_refs) - 1:
                s = s + bias_ref[0, h, off:off + nk, :]
            off += nk
            s_list.append(s)
        m = s_list[0].max(axis=0, keepdims=True)
        for s in s_list[1:]:
            m = jnp.maximum(m, s.max(axis=0, keepdims=True))
        l = jnp.zeros((1, tq), F32)
        acc = jnp.zeros((NA_HEAD_DIM, tq), F32)
        for s, vT_ref in zip(s_list, vT_refs):
            p = jnp.exp2(s - m)
            l = l + p.sum(axis=0, keepdims=True)
            acc = acc + jnp.dot(vT_ref[NA_HEAD_DIM * h:NA_HEAD_DIM * (h + 1), :], p.astype(BF16),
                                preferred_element_type=F32)
        outs.append(acc / l)
    o_ref[...] = jnp.concatenate(outs, axis=0).T.astype(BF16)


def _na_latent_kernel(q_ref, k0, k1, k2, kc, v0, v1, v2, vc, bias_ref, o_ref):
    _na_heads(q_ref, [k0, k1, k2, kc], [v0, v1, v2, vc], bias_ref, o_ref)


def _na_ctx_kernel(q_ref, kc, vc, o_ref):
    _na_heads(q_ref, [kc], [vc], None, o_ref)


def _na_latent(naqT, nak, navT, nakc, navcT, bias, layer):
    n = naqT.shape[1]
    nb = n // NA_TQ
    nctx = nakc.shape[0]

    def kb(o):
        return lambda i: (jnp.clip(i - 1, 0, nb - 3) + o, 0)

    def vb(o):
        return lambda i: (0, jnp.clip(i - 1, 0, nb - 3) + o)

    def variant(i):
        return (layer, jnp.where(i == 0, 0, jnp.where(i == nb - 1, 2, 1)), 0, 0, 0)

    return pl.pallas_call(
        _na_latent_kernel,
        grid=(nb,),
        in_specs=[
            pl.BlockSpec((NA_W, NA_TQ), lambda i: (0, i)),
            pl.BlockSpec((NA_TQ, NA_W), kb(0)),
            pl.BlockSpec((NA_TQ, NA_W), kb(1)),
            pl.BlockSpec((NA_TQ, NA_W), kb(2)),
            _const_spec((nctx, NA_W)),
            pl.BlockSpec((NA_W, NA_TQ), vb(0)),
            pl.BlockSpec((NA_W, NA_TQ), vb(1)),
            pl.BlockSpec((NA_W, NA_TQ), vb(2)),
            _const_spec((NA_W, nctx)),
            pl.BlockSpec((None, 1, NA_HEADS, NA_TK, NA_TQ), variant),
        ],
        out_specs=pl.BlockSpec((NA_TQ, NA_W), lambda i: (i, 0)),
        out_shape=jax.ShapeDtypeStruct((n, NA_W), BF16),
        compiler_params=_cparams(("parallel",)),
        name="na_latent",
    )(naqT, nak, nak, nak, nakc, navT, navT, navT, navcT, bias)


def _na_ctx(naqcT, nakc, navcT):
    nctx = nakc.shape[0]
    return pl.pallas_call(
        _na_ctx_kernel,
        grid=(1,),
        in_specs=[_const_spec((NA_W, nctx)), _const_spec((nctx, NA_W)), _const_spec((NA_W, nctx))],
        out_specs=pl.BlockSpec((nctx, NA_W), lambda i: (0, 0)),
        out_shape=jax.ShapeDtypeStruct((nctx, NA_W), BF16),
        compiler_params=_cparams(("arbitrary",)),
        name="na_ctx",
    )(naqcT, nakc, navcT)


def _na_bias_tables(rel_bias, rows):
    ndc = 2 * NA_WIN_W - 1
    col = np.arange(GRID_W)
    start_c = np.clip(col - NA_WIN_W // 2, 0, GRID_W - NA_WIN_W)
    in_col = (col[:, None] >= start_c[None, :]) & (col[:, None] < start_c[None, :] + NA_WIN_W)
    dc = col[:, None] - col[None, :] + (NA_WIN_W - 1)
    sel_c = (dc[:, :, None] == np.arange(ndc)).astype(np.float32)
    by_col = jnp.einsum('lhrc,kqc->lhrkq', rel_bias.astype(F32), sel_c, precision=lax.Precision.HIGHEST)
    by_col = jnp.where(in_col, by_col * LOG2E, NEG)
    masked = jnp.full(by_col.shape[:2] + (GRID_W, GRID_W), NEG, F32)
    tabs = []
    for r0, base in ((0, 0), (NA_QROWS, 0), (rows - NA_QROWS, rows - NA_KROWS)):
        key_rows = []
        for a in range(NA_KROWS):
            blocks = []
            for b in range(NA_QROWS):
                qr, kr = r0 + b, base + a
                start_r = min(max(qr - NA_WIN_H // 2, 0), rows - NA_WIN_H)
                inside = start_r <= kr < start_r + NA_WIN_H
                blocks.append(by_col[:, :, kr - qr + NA_WIN_H - 1] if inside else masked)
            key_rows.append(jnp.concatenate(blocks, axis=-1))
        tabs.append(jnp.concatenate(key_rows, axis=-2))
    return jnp.stack(tabs, axis=1)


def _mla_kernel(*refs, tk, has_ctx):
    if has_ctx:
        qT_ref, qn_ref, k_ref, vT_ref, kn_ref, kc_ref, vcT_ref, knc_ref, o_ref = refs[:9]
    else:
        qT_ref, qn_ref, k_ref, vT_ref, kn_ref, o_ref = refs[:6]
    m_ref, l_ref, acc_ref, al_ref, s_a, s_b, p_a, p_b, qs_ref = refs[-9:]
    tq = qT_ref.shape[1]
    nk = k_ref.shape[0] // tk

    kmax = jnp.max(kn_ref[0:1, :])
    if has_ctx:
        kmax = jnp.maximum(kmax, jnp.max(knc_ref[0:1, :]))
    bound = qn_ref[0:1, :] * (kmax * MLA_BOUND_MARGIN)
    shift_ok = jnp.max(bound) <= MLA_MAX_SHIFT

    @pl.when(shift_ok)
    def _():
        qs_ref[...] = qT_ref[...]
        row0 = lax.broadcasted_iota(jnp.int32, (16, tq), 0) == 0
        qs_ref[MLA_QK:MLA_QK + 16, :] = jnp.where(row0, -bound, 0.0).astype(BF16)

        def chunk(k_blk, vT_blk):
            p = jnp.exp2(jnp.dot(k_blk, qs_ref[...], preferred_element_type=F32))
            return p.sum(axis=0, keepdims=True), jnp.dot(vT_blk, p.astype(BF16), preferred_element_type=F32)

        if has_ctx:
            l0, a0 = chunk(kc_ref[...], vcT_ref[...])
            l_ref[...] = l0
            acc_ref[...] = a0
        else:
            l_ref[...] = jnp.zeros((1, tq), F32)
            acc_ref[...] = jnp.zeros((MLA_V, tq), F32)
        unroll = next(u for u in (8, 4, 2, 1) if nk % u == 0)

        def body(it, carry):
            l_sum = l_ref[...]
            a_sum = acc_ref[...]
            for u in range(unroll):
                k0 = pl.multiple_of((it * unroll + u) * tk, tk)
                lp, ap = chunk(k_ref[pl.ds(k0, tk), :], vT_ref[:, pl.ds(k0, tk)])
                l_sum = l_sum + lp
                a_sum = a_sum + ap
            l_ref[...] = l_sum
            acc_ref[...] = a_sum
            return carry

        lax.fori_loop(0, nk // unroll, body, 0)

    @pl.when(jnp.logical_not(shift_ok))
    def _():
        if has_ctx:
            s = jnp.dot(kc_ref[...], qT_ref[...], preferred_element_type=F32)
            m = s.max(axis=0, keepdims=True)
            p = jnp.exp2(s - m)
            m_ref[...] = m
            l_ref[...] = p.sum(axis=0, keepdims=True)
            acc_ref[...] = jnp.dot(vcT_ref[...], p.astype(BF16), preferred_element_type=F32)
        else:
            m_ref[...] = jnp.full((1, tq), NEG, F32)
            l_ref[...] = jnp.zeros((1, tq), F32)
            acc_ref[...] = jnp.zeros((MLA_V, tq), F32)

        def scores(chunk, s_ref):
            k0 = pl.multiple_of(chunk * tk, tk)
            s_ref[...] = jnp.dot(k_ref[pl.ds(k0, tk), :], qT_ref[...], preferred_element_type=F32)

        def softmax(s_ref, p_ref):
            s = s_ref[...]
            m_old = m_ref[...]
            m_new = jnp.maximum(m_old, s.max(axis=0, keepdims=True))
            alpha = jnp.exp2(m_old - m_new)
            p = jnp.exp2(s - m_new)
            l_ref[...] = alpha * l_ref[...] + p.sum(axis=0, keepdims=True)
            m_ref[...] = m_new
            p_ref[...] = p.astype(BF16)
            return alpha

        def values(chunk, p_ref, alpha):
            k0 = pl.multiple_of(chunk * tk, tk)
            acc_ref[...] = alpha * acc_ref[...] + jnp.dot(vT_ref[:, pl.ds(k0, tk)], p_ref[...],
                                                          preferred_element_type=F32)

        if nk % 2 == 0:
            scores(0, s_a)
            p_b[...] = jnp.zeros(p_b.shape, BF16)
            al_ref[...] = jnp.ones((1, tq), F32)

            def body(it, carry):
                c0 = 2 * it
                scores(c0 + 1, s_b)
                a0 = softmax(s_a, p_a)
                values(jnp.maximum(c0 - 1, 0), p_b, al_ref[...])
                scores(jnp.minimum(c0 + 2, nk - 1), s_a)
                a1 = softmax(s_b, p_b)
                values(c0, p_a, a0)
                al_ref[...] = a1
                return carry

            lax.fori_loop(0, nk // 2, body, 0)
            values(nk - 1, p_b, al_ref[...])
        else:
            for c in range(nk):
                scores(c, s_a)
                values(c, p_a, softmax(s_a, p_a))

    o_ref[...] = (acc_ref[...] / l_ref[...]).T.astype(BF16)


def _mla_attention(mqT, qn, mk, mvT, kn, ctx_kv, tq, tk):
    nq = mqT.shape[1]
    nkeys = mk.shape[0]
    has_ctx = ctx_kv is not None
    in_specs = [
        pl.BlockSpec((MLA_QPAD, tq), lambda h, i: (h, i)),
        pl.BlockSpec((8, tq), lambda h, i: (h, i)),
        pl.BlockSpec((nkeys, MLA_QPAD), lambda h, i: (0, h)),
        pl.BlockSpec((MLA_V, nkeys), lambda h, i: (h, 0)),
        pl.BlockSpec((8, nkeys), lambda h, i: (h, 0)),
    ]
    args = [mqT, qn, mk, mvT, kn]
    if has_ctx:
        kc, vcT, knc = ctx_kv
        nctx = kc.shape[0]
        in_specs += [
            pl.BlockSpec((nctx, MLA_QPAD), lambda h, i: (0, h)),
            pl.BlockSpec((MLA_V, nctx), lambda h, i: (h, 0)),
            pl.BlockSpec((8, nctx), lambda h, i: (h, 0)),
        ]
        args += [kc, vcT, knc]
    return pl.pallas_call(
        functools.partial(_mla_kernel, tk=tk, has_ctx=has_ctx),
        grid=(MLA_HEADS, nq // tq),
        in_specs=in_specs,
        out_specs=pl.BlockSpec((tq, MLA_V), lambda h, i: (i, h)),
        out_shape=jax.ShapeDtypeStruct((nq, MLA_HEADS * MLA_V), BF16),
        scratch_shapes=[pltpu.VMEM((1, tq), F32), pltpu.VMEM((1, tq), F32), pltpu.VMEM((MLA_V, tq), F32),
                        pltpu.VMEM((1, tq), F32),
                        pltpu.VMEM((tk, tq), F32), pltpu.VMEM((tk, tq), F32),
                        pltpu.VMEM((tk, tq), BF16), pltpu.VMEM((tk, tq), BF16),
                        pltpu.VMEM((MLA_QPAD, tq), BF16)],
        compiler_params=_cparams(("parallel", "arbitrary")),
        name="mla_attention",
    )(*args)


def _conv3_rows(x, prev_row, next_row, w):
    tm = x.shape[0]
    rows = lax.broadcasted_iota(jnp.int32, (tm, 1), 0)
    xm1 = jnp.where(rows == 0, prev_row, pltpu.roll(x, 1, 0))
    xp1 = jnp.where(rows == tm - 1, next_row, pltpu.roll(x, tm - 1, 0))
    return w[0:1, :] * xm1 + w[1:2, :] * x + w[2:3, :] * xp1


HALO = 16


def _merge_kernel(ya_ref, yb_ref, ubc_ref, prev_ref, next_ref, gate_ref, x_ref, g1_ref, cw_ref,
                  wa_ref, wb_ref, wc_ref, wo_ref, o_ref, *, row):
    i = pl.program_id(0)
    nblk = pl.num_programs(0)
    ubc = ubc_ref[...].astype(F32)
    u = ubc[:, :CONV_CH]
    gb = ubc[:, CONV_CH:2 * CONV_CH]
    gc = ubc[:, 2 * CONV_CH:]
    cu = gc * u
    pr = prev_ref[...].astype(F32)[HALO - 1:HALO, :]
    nx = next_ref[...].astype(F32)[0:1, :]
    cu_prev = pr[:, 2 * CONV_CH:] * pr[:, :CONV_CH] * (i > 0).astype(F32)
    cu_next = nx[:, 2 * CONV_CH:] * nx[:, :CONV_CH] * (i < nblk - 1).astype(F32)
    yc = (gb * _conv3_rows(cu, cu_prev, cu_next, cw_ref[...])).astype(BF16)

    a = jnp.dot(ya_ref[...], wa_ref[...], preferred_element_type=F32)
    b = jnp.dot(yb_ref[...], wb_ref[...], preferred_element_type=F32)
    c = jnp.dot(yc, wc_ref[...], preferred_element_type=F32)
    d = D_MODEL
    merged = (jax.nn.sigmoid(gate_ref[:, :d].astype(F32)) * a
              + jax.nn.sigmoid(gate_ref[:, d:2 * d].astype(F32)) * b
              + jax.nn.sigmoid(gate_ref[:, 2 * d:].astype(F32)) * c)
    mix = jnp.dot(merged.astype(BF16), wo_ref[...], preferred_element_type=F32)
    o_ref[...] = x_ref[...] + g1_ref[row:row + 1, :] * mix


def _merge(ya, yb, pg, x2d, mod, g1_blk, row, conv_w8, wa, wb, wc, wo, layer, tm):
    n, d = x2d.shape
    ubc_blk = (PG_P + P_UBC) // (3 * CONV_CH)
    hb = tm // HALO
    last = n // HALO - 1
    return pl.pallas_call(
        functools.partial(_merge_kernel, row=row),
        grid=(n // tm,),
        in_specs=[
            pl.BlockSpec((tm, NA_W), lambda i: (i, 0)),
            pl.BlockSpec((tm, MLA_HEADS * MLA_V), lambda i: (i, 0)),
            pl.BlockSpec((tm, 3 * CONV_CH), lambda i: (i, ubc_blk)),
            pl.BlockSpec((HALO, 3 * CONV_CH), lambda i: (jnp.maximum(i * hb - 1, 0), ubc_blk)),
            pl.BlockSpec((HALO, 3 * CONV_CH), lambda i: (jnp.minimum((i + 1) * hb, last), ubc_blk)),
            pl.BlockSpec((tm, G_COLS), lambda i: (i, 0)),
            pl.BlockSpec((tm, d), lambda i: (i, 0)),
            pl.BlockSpec((8, d), lambda i: (0, g1_blk), pipeline_mode=pl.Buffered(1)),
            _layer_spec((8, CONV_CH), layer),
            _layer_spec(wa.shape[1:], layer),
            _layer_spec(wb.shape[1:], layer),
            _layer_spec(wc.shape[1:], layer),
            _layer_spec(wo.shape[1:], layer),
        ],
        out_specs=pl.BlockSpec((tm, d), lambda i: (i, 0)),
        out_shape=jax.ShapeDtypeStruct((n, d), F32),
        compiler_params=_cparams(("parallel",)),
        name="merge_out",
    )(ya, yb, pg, pg, pg, pg, x2d, mod, conv_w8, wa, wb, wc, wo)


FFN_KC = 256


def _ffn_down_kernel(a_ref, prev_ref, next_ref, val_ref, cw_ref, wd_ref, x_ref, g2_ref, o_ref,
                     acc_ref, h_a, h_b, *, row):
    i = pl.program_id(0)
    nblk = pl.num_programs(0)
    has_prev = (i > 0).astype(F32)
    has_next = (i < nblk - 1).astype(F32)
    nchunks = D_FF // FFN_KC

    def hidden(c, h_ref):
        sl = slice(c * FFN_KC, (c + 1) * FFN_KC)
        w = cw_ref[:, sl]
        a = a_ref[:, sl].astype(F32)
        pr = prev_ref[:, sl].astype(F32)[HALO - 1:HALO, :] * has_prev
        nx = next_ref[:, sl].astype(F32)[0:1, :] * has_next
        a = _conv3_rows(a, pr, nx, w) + w[3:4, :]
        h_ref[...] = (a * jax.nn.sigmoid(a) * val_ref[:, sl].astype(F32)).astype(BF16)

    bufs = (h_a, h_b)
    hidden(0, h_a)
    for c in range(nchunks):
        if c + 1 < nchunks:
            hidden(c + 1, bufs[(c + 1) % 2])
        part = jnp.dot(bufs[c % 2][...], wd_ref[c * FFN_KC:(c + 1) * FFN_KC, :], preferred_element_type=F32)
        if c == 0:
            acc_ref[...] = part
        else:
            acc_ref[...] += part
    o_ref[...] = x_ref[...] + g2_ref[row:row + 1, :] * acc_ref[...]


def _ffn_down(av, x2d, mod, g2_blk, row, cw8, wd, layer, tm):
    n, d = x2d.shape
    hb = tm // HALO
    last = n // HALO - 1
    return pl.pallas_call(
        functools.partial(_ffn_down_kernel, row=row),
        grid=(n // tm,),
        in_specs=[
            pl.BlockSpec((tm, D_FF), lambda i: (i, 0)),
            pl.BlockSpec((HALO, D_FF), lambda i: (jnp.maximum(i * hb - 1, 0), 0)),
            pl.BlockSpec((HALO, D_FF), lambda i: (jnp.minimum((i + 1) * hb, last), 0)),
            pl.BlockSpec((tm, D_FF), lambda i: (i, 1)),
            _layer_spec((8, D_FF), layer),
            _layer_spec((D_FF, d), layer),
            pl.BlockSpec((tm, d), lambda i: (i, 0)),
            pl.BlockSpec((8, d), lambda i: (0, g2_blk), pipeline_mode=pl.Buffered(1)),
        ],
        out_specs=pl.BlockSpec((tm, d), lambda i: (i, 0)),
        out_shape=jax.ShapeDtypeStruct((n, d), F32),
        scratch_shapes=[pltpu.VMEM((tm, d), F32), pltpu.VMEM((tm, FFN_KC), BF16), pltpu.VMEM((tm, FFN_KC), BF16)],
        compiler_params=_cparams(("parallel",)),
        name="ffn_down",
    )(av, av, av, av, cw8, wd, x2d, mod)


def _layout_w_in(w):
    z64 = jnp.zeros(w.shape[:-1] + (64,), w.dtype)
    cols = [w[..., R_GATE:], w[..., R_NA_K:R_NA_V], w[..., R_NA_V:R_CKV], w[..., R_NA_Q:R_MLA_Q]]
    for h in range(MLA_HEADS):
        cols += [w[..., R_MLA_Q + h * MLA_QK:R_MLA_Q + (h + 1) * MLA_QK], z64]
    cols += [w[..., R_CKV:R_KROPE], w[..., R_KROPE:R_NA_Q], z64, z64, z64, w[..., R_CONV_U:R_GATE]]
    return jnp.concatenate(cols, axis=-1).astype(BF16)


def _pad_rows(a, rows):
    return jnp.concatenate([a, jnp.zeros((a.shape[0], rows - a.shape[1], a.shape[2]), a.dtype)], axis=1)


def _rope_tables_T(n_tok):
    t = jnp.arange(n_tok)
    row = (t // GRID_W).astype(F32)
    col = (t % GRID_W).astype(F32)
    inv = ROPE_BASE ** (-jnp.arange(ROPE_NFREQ, dtype=F32) / ROPE_NFREQ)
    ang = jnp.stack([inv[:, None] * row[None, :], inv[:, None] * col[None, :]], axis=0)
    return jnp.cos(ang), jnp.sin(ang)


def kernel(x, c, ctx, c_ctx, w_mod, b_mod, norm1_g, w_in, na_q_g, na_k_g, na_rel_bias, mla_kv_g, w_kv_up,
           mla_q_g, mla_k_g, conv_w, w_br_a, w_br_b, w_br_c, w_o, norm2_g, w_up, ffn_conv_w, ffn_conv_b, w_down):
    bsz, seq, d = x.shape
    nctx = ctx.shape[1]
    depth = w_mod.shape[0]
    rows = seq // GRID_W
    assert bsz == 1 and d == D_MODEL and seq % NA_TQ == 0 and rows >= NA_KROWS and nctx % 256 == 0

    xs = x[0]
    cs = ctx[0]
    tm_x = min(1024, seq)
    tm_half = min(512, seq)
    tm_merge = min(256, seq)
    tq_mla = min(1024, seq)
    tk_mla = min(1024, seq)

    cos_x, sin_x = _rope_tables_T(seq)
    cos_c = jnp.ones((2, ROPE_NFREQ, nctx), F32)
    sin_c = jnp.zeros((2, ROPE_NFREQ, nctx), F32)
    c_cols = jnp.zeros((d, 128), F32).at[:, 0].set(c[0]).at[:, 1].set(c_ctx)

    w_in_l = [_layout_w_in(w_in[l:l + 1]) for l in range(depth)]
    kv_up = w_kv_up.reshape(depth, MLA_KV_RANK, MLA_HEADS, MLA_NOPE + MLA_V)
    wknT = jnp.swapaxes(kv_up[..., :MLA_NOPE].reshape(depth, MLA_KV_RANK, -1), 1, 2).astype(BF16)
    wvT = jnp.swapaxes(kv_up[..., MLA_NOPE:].reshape(depth, MLA_KV_RANK, -1), 1, 2).astype(BF16)
    gains = (na_q_g[..., None], na_k_g[..., None], mla_kv_g[..., None],
             _pad_rows(mla_q_g[..., None], MLA_QPAD), _pad_rows(mla_k_g[..., None], MLA_QPAD))
    bias = _na_bias_tables(na_rel_bias, rows)
    conv_w8 = _pad_rows(conv_w, 8)
    ffn_cw8 = _pad_rows(jnp.concatenate([ffn_conv_w, ffn_conv_b[:, None, :]], axis=1), 8)
    wa, wb, wc, wo, w_up_all, w_down_all = (w.astype(BF16) for w in (w_br_a, w_br_b, w_br_c, w_o, w_up, w_down))
    n1g = norm1_g[:, None, :]
    n2g = norm2_g[:, None, :]

    for l in range(depth):
        last = l == depth - 1
        mod = _modulation(c_cols, w_mod, b_mod, l)

        pgc = _norm_matmul(cs, n1g[l:l + 1], mod, 0, 1, w_in_l[l], 0, nctx, IN_TN, "in_proj")
        naqcT, nakc, navcT, mqcT, mkc, mvcT, qnc, knc = _prep(pgc, cos_c, sin_c, gains, wknT, wvT, l, nctx)
        pgx = _norm_matmul(xs, n1g[l:l + 1], mod, 0, 0, w_in_l[l], 0, tm_x, IN_TN, "in_proj")
        naqT, nak, navT, mqT, mk, mvT, qn, kn = _prep(pgx, cos_x, sin_x, gains, wknT, wvT, l, tm_half)

        ya = _na_latent(naqT, nak, navT, nakc, navcT, bias, l)
        yb = _mla_attention(mqT, qn, mk, mvT, kn, (mkc, mvcT, knc), tq_mla, tk_mla)
        xs = _merge(ya, yb, pgx, xs, mod, 2, 0, conv_w8, wa, wb, wc, wo, l, tm_merge)
        av = _norm_matmul(xs, n2g, mod, 3, 0, w_up_all, l, tm_x, UP_TN, "ffn_up")
        xs = _ffn_down(av, xs, mod, 5, 0, ffn_cw8, w_down_all, l, tm_merge)

        if not last:
            yac = _na_ctx(naqcT, nakc, navcT)
            ybc = _mla_attention(mqcT, qnc, mkc, mvcT, knc, None, nctx, nctx)
            cs = _merge(yac, ybc, pgc, cs, mod, 2, 1, conv_w8, wa, wb, wc, wo, l, nctx)
            avc = _norm_matmul(cs, n2g, mod, 3, 1, w_up_all, l, nctx, UP_TN, "ffn_up")
            cs = _ffn_down(avc, cs, mod, 5, 1, ffn_cw8, w_down_all, l, nctx)

    return xs[None]
```

```python
import functools

import math

import jax
import jax.numpy as jnp
import numpy as np
from jax import lax
from jax.experimental import pallas as pl
from jax.experimental.pallas import tpu as pltpu

F32 = jnp.float32
BF16 = jnp.bfloat16

D_MODEL = 2048
GRID_W = 64
EPS = 1e-6
NA_HEADS = 8
NA_HEAD_DIM = 64
NA_WIN_H = 8
NA_WIN_W = 16
NA_W = NA_HEADS * NA_HEAD_DIM
MLA_HEADS = 4
MLA_NOPE = 128
MLA_ROPE = 64
MLA_V = 128
MLA_KV_RANK = 256
MLA_QK = MLA_NOPE + MLA_ROPE
MLA_QPAD = 256
ROPE_NFREQ = MLA_ROPE // 4
ROPE_BASE = 10000.0
CONV_CH = 512
D_FF = 5632

R_NA_K = 0
R_NA_V = R_NA_K + NA_W
R_CKV = R_NA_V + NA_W
R_KROPE = R_CKV + MLA_KV_RANK
R_NA_Q = R_KROPE + MLA_ROPE
R_MLA_Q = R_NA_Q + NA_W
R_CONV_U = R_MLA_Q + MLA_HEADS * MLA_QK
R_GATE = R_CONV_U + 3 * CONV_CH

LANE = 128
OFF = R_NA_Q % LANE
PG_COLS = 10752
QKV_WIN = 3072
NAQ_WIN = (R_NA_Q - OFF, R_MLA_Q + OFF)
MQ_WIN = (R_MLA_Q - OFF, R_CONV_U + OFF)
CONV_WIN0 = R_CONV_U - OFF
CONV_WIN = 2560
CONV_W = CONV_CH + 2 * OFF
GATE_WIN0 = R_GATE - OFF
G_COLS = 3 * D_MODEL

NA_QROWS = 4
NA_KROWS = 12
NA_TQ = NA_QROWS * GRID_W
NA_TK = NA_KROWS * GRID_W
NEG = -1e30
LOG2E = math.log2(math.e)
MLA_BOUND_MARGIN = 1.02
MLA_MAX_SHIFT = 50.0

VMEM_LIMIT = 56 * 1024 * 1024


def _cparams(sem):
    return pltpu.CompilerParams(dimension_semantics=sem, vmem_limit_bytes=VMEM_LIMIT)


def _const_spec(shape):
    nd = len(shape)
    return pl.BlockSpec(shape, lambda *_: (0,) * nd, pipeline_mode=pl.Buffered(1))


def _layer_spec(shape, layer):
    nd = len(shape)
    return pl.BlockSpec((None,) + tuple(shape), lambda *_: (layer,) + (0,) * nd, pipeline_mode=pl.Buffered(1))


MOD_KC = 256


def _mod_kernel(c_ref, w_ref, b_ref, o_ref):
    tn = o_ref.shape[1]
    nk = w_ref.shape[0] // MOD_KC

    def body(kc, acc):
        k0 = pl.multiple_of(kc * MOD_KC, MOD_KC)
        w = w_ref[pl.ds(k0, MOD_KC), :]
        c = c_ref[pl.ds(k0, MOD_KC), :]
        s = c * jax.nn.sigmoid(c)
        out = []
        for r in range(2):
            prod = s[:, r:r + 1] * w
            out.append(acc[r] + jnp.sum(prod.reshape(MOD_KC // 8, 8, tn), axis=0))
        return tuple(out)

    z = jnp.zeros((8, tn), F32)
    acc = lax.fori_loop(0, nk, body, (z, z))
    r0, r1 = [jnp.sum(a, axis=0, keepdims=True) + b_ref[...] for a in acc]
    ridx = lax.broadcasted_iota(jnp.int32, (8, tn), 0)
    o_ref[...] = jnp.where(ridx == 0, r0, jnp.where(ridx == 1, r1, 0.0))


def _modulation(c_cols, w_mod, b_mod, layer):
    depth, d, n = w_mod.shape
    tn = 1536
    return pl.pallas_call(
        _mod_kernel,
        grid=(n // tn,),
        in_specs=[
            _const_spec((d, 128)),
            pl.BlockSpec((None, d, tn), lambda j: (layer, 0, j)),
            pl.BlockSpec((None, 1, tn), lambda j: (layer, 0, j)),
        ],
        out_specs=pl.BlockSpec((8, tn), lambda j: (0, j)),
        out_shape=jax.ShapeDtypeStruct((8, n), F32),
        compiler_params=_cparams(("arbitrary",)),
        name="modulation",
    )(c_cols, w_mod, b_mod.reshape(depth, 1, n))


NORM_RC = 128
IN_TN = 1536
UP_TN = 1024


def _norm_rows(x_ref, g_ref, sh_ref, sc_ref, hx_ref, row):
    tm = x_ref.shape[0]
    g = g_ref[...]
    sc1 = 1.0 + sc_ref[row:row + 1, :]
    sh = sh_ref[row:row + 1, :]

    def body(r, carry):
        r0 = pl.multiple_of(r * NORM_RC, NORM_RC)
        x = x_ref[pl.ds(r0, NORM_RC), :]
        ms = jnp.mean(x * x, axis=-1, keepdims=True)
        y = (x * lax.rsqrt(ms + EPS)) * g
        hx_ref[pl.ds(r0, NORM_RC), :] = (y * sc1 + sh).astype(BF16)
        return carry

    lax.fori_loop(0, tm // NORM_RC, body, 0)


def _norm_matmul_kernel(x_ref, g_ref, sh_ref, sc_ref, w_ref, o_ref, hx_ref, *, row):
    @pl.when(pl.program_id(1) == 0)
    def _():
        _norm_rows(x_ref, g_ref, sh_ref, sc_ref, hx_ref, row)

    o_ref[...] = jnp.dot(hx_ref[...], w_ref[...], preferred_element_type=F32).astype(BF16)


def _norm_matmul(x2d, norm_g, mod, sh_blk, row, w_all, layer, tm, tn, name):
    n, d = x2d.shape
    ncol = w_all.shape[2]
    return pl.pallas_call(
        functools.partial(_norm_matmul_kernel, row=row),
        grid=(n // tm, ncol // tn),
        in_specs=[
            pl.BlockSpec((tm, d), lambda i, j: (i, 0)),
            _layer_spec((1, d), layer),
            pl.BlockSpec((8, d), lambda i, j: (0, sh_blk), pipeline_mode=pl.Buffered(1)),
            pl.BlockSpec((8, d), lambda i, j: (0, sh_blk + 1), pipeline_mode=pl.Buffered(1)),
            pl.BlockSpec((None, d, tn), lambda i, j: (layer, 0, j)),
        ],
        out_specs=pl.BlockSpec((tm, tn), lambda i, j: (i, j)),
        out_shape=jax.ShapeDtypeStruct((n, ncol), BF16),
        scratch_shapes=[pltpu.VMEM((tm, d), BF16)],
        compiler_params=_cparams(("parallel", "arbitrary")),
        name=name,
    )(x2d, norm_g, mod, mod, w_all)


def _rope_rows(r, cos_ref, sin_ref):
    out = []
    for part in range(2):
        cs = cos_ref[part]
        sn = sin_ref[part]
        x1 = r[32 * part:32 * part + 16]
        x2 = r[32 * part + 16:32 * part + 32]
        out.append(x1 * cs - x2 * sn)
        out.append(x2 * cs + x1 * sn)
    return out


def _prep_kernel(p_ref, cos_ref, sin_ref, naqg_ref, nakg_ref, kvg_ref, mqg_ref, mkg_ref,
                 wkn_ref, wv_ref,
                 naqT_ref, nak_ref, navT_ref, mqT_ref, mk_ref, mvT_ref, qn_ref, kn_ref):
    tm = p_ref.shape[0]

    def head_norm_T(xt, g_ref, scale):
        x3 = xt.reshape(NA_HEADS, NA_HEAD_DIM, tm)
        ms = jnp.mean(x3 * x3, axis=1, keepdims=True)
        y = x3 * lax.rsqrt(ms + EPS) * g_ref[...][None]
        if scale != 1.0:
            y = y * scale
        return y.reshape(NA_W, tm)

    naq_t = p_ref[:, NAQ_WIN[0]:NAQ_WIN[1]].astype(F32).T[OFF:OFF + NA_W]
    naqT_ref[...] = head_norm_T(naq_t, naqg_ref, NA_HEAD_DIM ** -0.5 * LOG2E).astype(BF16)
    nak_t = p_ref[:, R_NA_K:R_NA_V].astype(F32).T
    nak_ref[...] = head_norm_T(nak_t, nakg_ref, 1.0).T.astype(BF16)
    navT_ref[...] = p_ref[:, R_NA_V:R_CKV].astype(F32).T.astype(BF16)

    mqg = mqg_ref[...][:MLA_QK]
    qscale = MLA_QK ** -0.5 * LOG2E
    mq_t = p_ref[:, MQ_WIN[0]:MQ_WIN[1]].astype(F32).T
    for h in range(MLA_HEADS):
        qt = mq_t[OFF + h * MLA_QK:OFF + (h + 1) * MLA_QK]
        ms = jnp.sum(qt * qt, axis=0, keepdims=True) * (1.0 / MLA_QK)
        y = qt * lax.rsqrt(ms + EPS) * mqg
        r0 = h * MLA_QPAD
        mqT_ref[r0:r0 + MLA_NOPE, :] = (y[:MLA_NOPE] * qscale).astype(BF16)
        parts = _rope_rows(y[MLA_NOPE:], cos_ref, sin_ref)
        for k, part in enumerate(parts):
            a = r0 + MLA_NOPE + 16 * k
            mqT_ref[a:a + 16, :] = (part * qscale).astype(BF16)
        mqT_ref[r0 + MLA_QK:r0 + MLA_QPAD, :] = jnp.zeros((MLA_QPAD - MLA_QK, tm), BF16)
        qnorm = jnp.sqrt(jnp.sum(y * y, axis=0, keepdims=True)) * qscale
        qn_ref[8 * h:8 * h + 8, :] = jnp.broadcast_to(qnorm, (8, tm))

    ct = p_ref[:, R_CKV:R_KROPE].astype(F32).T
    ms = jnp.mean(ct * ct, axis=0, keepdims=True)
    cn = (ct * lax.rsqrt(ms + EPS) * kvg_ref[...]).astype(BF16)
    mvT_ref[...] = jnp.dot(wv_ref[...], cn, preferred_element_type=F32).astype(BF16)
    knT = jnp.dot(wkn_ref[...], cn, preferred_element_type=F32)
    krT = p_ref[:, R_KROPE:R_KROPE + LANE].astype(F32).T[:MLA_ROPE]
    kr_ss = jnp.sum(krT * krT, axis=0, keepdims=True)
    mkg = mkg_ref[...]
    pad = (lax.broadcasted_iota(jnp.int32, (MLA_QPAD - MLA_QK, tm), 0) == 0).astype(F32)
    for h in range(MLA_HEADS):
        kn = knT[h * MLA_NOPE:(h + 1) * MLA_NOPE]
        ms = (jnp.sum(kn * kn, axis=0, keepdims=True) + kr_ss) * (1.0 / MLA_QK)
        rinv = lax.rsqrt(ms + EPS)
        kn = kn * rinv * mkg[:MLA_NOPE]
        kr = krT * rinv * mkg[MLA_NOPE:MLA_QK]
        knorm = jnp.sqrt(jnp.sum(kn * kn, axis=0, keepdims=True) + jnp.sum(kr * kr, axis=0, keepdims=True))
        kn_ref[8 * h:8 * h + 8, :] = jnp.broadcast_to(knorm, (8, tm))
        pieces = [kn] + _rope_rows(kr, cos_ref, sin_ref) + [pad]
        kT = jnp.concatenate(pieces, axis=0)
        mk_ref[:, h * MLA_QPAD:(h + 1) * MLA_QPAD] = kT.T.astype(BF16)


def _prep(pg, cosT, sinT, gains, wknT, wvT, layer, tm):
    n = pg.shape[0]
    naqg, nakg, kvg, mqg, mkg = gains
    return pl.pallas_call(
        _prep_kernel,
        grid=(n // tm,),
        in_specs=[
            pl.BlockSpec((tm, QKV_WIN), lambda i: (i, 0)),
            pl.BlockSpec((2, ROPE_NFREQ, tm), lambda i: (0, 0, i)),
            pl.BlockSpec((2, ROPE_NFREQ, tm), lambda i: (0, 0, i)),
            _layer_spec((NA_HEAD_DIM, 1), layer),
            _layer_spec((NA_HEAD_DIM, 1), layer),
            _layer_spec((MLA_KV_RANK, 1), layer),
            _layer_spec((MLA_QPAD, 1), layer),
            _layer_spec((MLA_QPAD, 1), layer),
            _layer_spec((MLA_HEADS * MLA_NOPE, MLA_KV_RANK), layer),
            _layer_spec((MLA_HEADS * MLA_V, MLA_KV_RANK), layer),
        ],
        out_specs=[
            pl.BlockSpec((NA_W, tm), lambda i: (0, i)),
            pl.BlockSpec((tm, NA_W), lambda i: (i, 0)),
            pl.BlockSpec((NA_W, tm), lambda i: (0, i)),
            pl.BlockSpec((MLA_HEADS * MLA_QPAD, tm), lambda i: (0, i)),
            pl.BlockSpec((tm, MLA_HEADS * MLA_QPAD), lambda i: (i, 0)),
            pl.BlockSpec((MLA_HEADS * MLA_V, tm), lambda i: (0, i)),
            pl.BlockSpec((MLA_HEADS * 8, tm), lambda i: (0, i)),
            pl.BlockSpec((MLA_HEADS * 8, tm), lambda i: (0, i)),
        ],
        out_shape=[
            jax.ShapeDtypeStruct((NA_W, n), BF16),
            jax.ShapeDtypeStruct((n, NA_W), BF16),
            jax.ShapeDtypeStruct((NA_W, n), BF16),
            jax.ShapeDtypeStruct((MLA_HEADS * MLA_QPAD, n), BF16),
            jax.ShapeDtypeStruct((n, MLA_HEADS * MLA_QPAD), BF16),
            jax.ShapeDtypeStruct((MLA_HEADS * MLA_V, n), BF16),
            jax.ShapeDtypeStruct((MLA_HEADS * 8, n), F32),
            jax.ShapeDtypeStruct((MLA_HEADS * 8, n), F32),
        ],
        compiler_params=_cparams(("parallel",)),
        name="qkv_prep",
    )(pg, cosT, sinT, naqg, nakg, kvg, mqg, mkg, wknT, wvT)


def _na_heads(q_ref, k_refs, vT_refs, bias_ref, o_ref):
    tq = q_ref.shape[1]
    upper = lax.broadcasted_iota(jnp.int32, (128, tq), 0) >= NA_HEAD_DIM
    outs = []
    for h in range(NA_HEADS):
        j, e = divmod(h, 2)
        q2 = q_ref[128 * j:128 * j + 128, :]
        qm = jnp.where(upper == bool(e), q2, jnp.zeros_like(q2))
        s_list = []
        off = 0
        for b, k_ref in enumerate(k_refs):
            s = jnp.dot(k_ref[:, 128 * j:128 * j + 128], qm, preferred_element_type=F32)
            nk = k_ref.shape[0]
            if bias_ref is not None and b < len(k_refs) - 1:
                s = s + bias_ref[0, h, off:off + nk, :]
            off += nk
            s_list.append(s)
        m = s_list[0].max(axis=0, keepdims=True)
        for s in s_list[1:]:
            m = jnp.maximum(m, s.max(axis=0, keepdims=True))
        l = jnp.zeros((1, tq), F32)
        acc = jnp.zeros((NA_HEAD_DIM, tq), F32)
        for s, vT_ref in zip(s_list, vT_refs):
            p = jnp.exp2(s - m)
            l = l + p.sum(axis=0, keepdims=True)
            acc = acc + jnp.dot(vT_ref[NA_HEAD_DIM * h:NA_HEAD_DIM * (h + 1), :], p.astype(BF16),
                                preferred_element_type=F32)
        outs.append(acc / l)
    o_ref[...] = jnp.concatenate(outs, axis=0).T.astype(BF16)


def _na_latent_kernel(q_ref, k0, k1, k2, kc, v0, v1, v2, vc, bias_ref, o_ref):
    _na_heads(q_ref, [k0, k1, k2, kc], [v0, v1, v2, vc], bias_ref, o_ref)


def _na_ctx_kernel(q_ref, kc, vc, o_ref):
    _na_heads(q_ref, [kc], [vc], None, o_ref)


def _na_latent(naqT, nak, navT, nakc, navcT, bias, layer):
    n = naqT.shape[1]
    nb = n // NA_TQ
    nctx = nakc.shape[0]

    def kb(o):
        return lambda i: (jnp.clip(i - 1, 0, nb - 3) + o, 0)

    def vb(o):
        return lambda i: (0, jnp.clip(i - 1, 0, nb - 3) + o)

    def variant(i):
        return (layer, jnp.where(i == 0, 0, jnp.where(i == nb - 1, 2, 1)), 0, 0, 0)

    return pl.pallas_call(
        _na_latent_kernel,
        grid=(nb,),
        in_specs=[
            pl.BlockSpec((NA_W, NA_TQ), lambda i: (0, i)),
            pl.BlockSpec((NA_TQ, NA_W), kb(0)),
            pl.BlockSpec((NA_TQ, NA_W), kb(1)),
            pl.BlockSpec((NA_TQ, NA_W), kb(2)),
            _const_spec((nctx, NA_W)),
            pl.BlockSpec((NA_W, NA_TQ), vb(0)),
            pl.BlockSpec((NA_W, NA_TQ), vb(1)),
            pl.BlockSpec((NA_W, NA_TQ), vb(2)),
            _const_spec((NA_W, nctx)),
            pl.BlockSpec((None, 1, NA_HEADS, NA_TK, NA_TQ), variant),
        ],
        out_specs=pl.BlockSpec((NA_TQ, NA_W), lambda i: (i, 0)),
        out_shape=jax.ShapeDtypeStruct((n, NA_W), BF16),
        compiler_params=_cparams(("parallel",)),
        name="na_latent",
    )(naqT, nak, nak, nak, nakc, navT, navT, navT, navcT, bias)


def _na_ctx(naqcT, nakc, navcT):
    nctx = nakc.shape[0]
    return pl.pallas_call(
        _na_ctx_kernel,
        grid=(1,),
        in_specs=[_const_spec((NA_W, nctx)), _const_spec((nctx, NA_W)), _const_spec((NA_W, nctx))],
        out_specs=pl.BlockSpec((nctx, NA_W), lambda i: (0, 0)),
        out_shape=jax.ShapeDtypeStruct((nctx, NA_W), BF16),
        compiler_params=_cparams(("arbitrary",)),
        name="na_ctx",
    )(naqcT, nakc, navcT)


def _na_bias_tables(rel_bias, rows):
    ndc = 2 * NA_WIN_W - 1
    col = np.arange(GRID_W)
    start_c = np.clip(col - NA_WIN_W // 2, 0, GRID_W - NA_WIN_W)
    in_col = (col[:, None] >= start_c[None, :]) & (col[:, None] < start_c[None, :] + NA_WIN_W)
    dc = col[:, None] - col[None, :] + (NA_WIN_W - 1)
    sel_c = (dc[:, :, None] == np.arange(ndc)).astype(np.float32)
    by_col = jnp.einsum('lhrc,kqc->lhrkq', rel_bias.astype(F32), sel_c, precision=lax.Precision.HIGHEST)
    by_col = jnp.where(in_col, by_col * LOG2E, NEG)
    masked = jnp.full(by_col.shape[:2] + (GRID_W, GRID_W), NEG, F32)
    tabs = []
    for r0, base in ((0, 0), (NA_QROWS, 0), (rows - NA_QROWS, rows - NA_KROWS)):
        key_rows = []
        for a in range(NA_KROWS):
            blocks = []
            for b in range(NA_QROWS):
                qr, kr = r0 + b, base + a
                start_r = min(max(qr - NA_WIN_H // 2, 0), rows - NA_WIN_H)
                inside = start_r <= kr < start_r + NA_WIN_H
                blocks.append(by_col[:, :, kr - qr + NA_WIN_H - 1] if inside else masked)
            key_rows.append(jnp.concatenate(blocks, axis=-1))
        tabs.append(jnp.concatenate(key_rows, axis=-2))
    return jnp.stack(tabs, axis=1)


def _mla_kernel(*refs, tk, has_ctx):
    if has_ctx:
        qT_ref, qn_ref, k_ref, vT_ref, kn_ref, kc_ref, vcT_ref, knc_ref, o_ref = refs[:9]
    else:
        qT_ref, qn_ref, k_ref, vT_ref, kn_ref, o_ref = refs[:6]
    m_ref, l_ref, acc_ref, al_ref, s_a, s_b, p_a, p_b, qs_ref = refs[-9:]
    tq = qT_ref.shape[1]
    nk = k_ref.shape[0] // tk

    kmax = jnp.max(kn_ref[0:1, :])
    if has_ctx:
        kmax = jnp.maximum(kmax, jnp.max(knc_ref[0:1, :]))
    bound = qn_ref[0:1, :] * (kmax * MLA_BOUND_MARGIN)
    shift_ok = jnp.max(bound) <= MLA_MAX_SHIFT

    @pl.when(shift_ok)
    def _():
        qs_ref[...] = qT_ref[...]
        row0 = lax.broadcasted_iota(jnp.int32, (16, tq), 0) == 0
        qs_ref[MLA_QK:MLA_QK + 16, :] = jnp.where(row0, -bound, 0.0).astype(BF16)

        def chunk(k_blk, vT_blk):
            p = jnp.exp2(jnp.dot(k_blk, qs_ref[...], preferred_element_type=F32))
            return p.sum(axis=0, keepdims=True), jnp.dot(vT_blk, p.astype(BF16), preferred_element_type=F32)

        if has_ctx:
            l0, a0 = chunk(kc_ref[...], vcT_ref[...])
            l_ref[...] = l0
            acc_ref[...] = a0
        else:
            l_ref[...] = jnp.zeros((1, tq), F32)
            acc_ref[...] = jnp.zeros((MLA_V, tq), F32)
        unroll = next(u for u in (8, 4, 2, 1) if nk % u == 0)

        def body(it, carry):
            l_sum = l_ref[...]
            a_sum = acc_ref[...]
            for u in range(unroll):
                k0 = pl.multiple_of((it * unroll + u) * tk, tk)
                lp, ap = chunk(k_ref[pl.ds(k0, tk), :], vT_ref[:, pl.ds(k0, tk)])
                l_sum = l_sum + lp
                a_sum = a_sum + ap
            l_ref[...] = l_sum
            acc_ref[...] = a_sum
            return carry

        lax.fori_loop(0, nk // unroll, body, 0)

    @pl.when(jnp.logical_not(shift_ok))
    def _():
        if has_ctx:
            s = jnp.dot(kc_ref[...], qT_ref[...], preferred_element_type=F32)
            m = s.max(axis=0, keepdims=True)
            p = jnp.exp2(s - m)
            m_ref[...] = m
            l_ref[...] = p.sum(axis=0, keepdims=True)
            acc_ref[...] = jnp.dot(vcT_ref[...], p.astype(BF16), preferred_element_type=F32)
        else:
            m_ref[...] = jnp.full((1, tq), NEG, F32)
            l_ref[...] = jnp.zeros((1, tq), F32)
            acc_ref[...] = jnp.zeros((MLA_V, tq), F32)

        def scores(chunk, s_ref):
            k0 = pl.multiple_of(chunk * tk, tk)
            s_ref[...] = jnp.dot(k_ref[pl.ds(k0, tk), :], qT_ref[...], preferred_element_type=F32)

        def softmax(s_ref, p_ref):
            s = s_ref[...]
            m_old = m_ref[...]
            m_new = jnp.maximum(m_old, s.max(axis=0, keepdims=True))
            alpha = jnp.exp2(m_old - m_new)
            p = jnp.exp2(s - m_new)
            l_ref[...] = alpha * l_ref[...] + p.sum(axis=0, keepdims=True)
            m_ref[...] = m_new
            p_ref[...] = p.astype(BF16)
            return alpha

        def values(chunk, p_ref, alpha):
            k0 = pl.multiple_of(chunk * tk, tk)
            acc_ref[...] = alpha * acc_ref[...] + jnp.dot(vT_ref[:, pl.ds(k0, tk)], p_ref[...],
                                                          preferred_element_type=F32)

        if nk % 2 == 0:
            scores(0, s_a)
            p_b[...] = jnp.zeros(p_b.shape, BF16)
            al_ref[...] = jnp.ones((1, tq), F32)

            def body(it, carry):
                c0 = 2 * it
                scores(c0 + 1, s_b)
                a0 = softmax(s_a, p_a)
                values(jnp.maximum(c0 - 1, 0), p_b, al_ref[...])
                scores(jnp.minimum(c0 + 2, nk - 1), s_a)
                a1 = softmax(s_b, p_b)
                values(c0, p_a, a0)
                al_ref[...] = a1
                return carry

            lax.fori_loop(0, nk // 2, body, 0)
            values(nk - 1, p_b, al_ref[...])
        else:
            for c in range(nk):
                scores(c, s_a)
                values(c, p_a, softmax(s_a, p_a))

    o_ref[...] = (acc_ref[...] / l_ref[...]).T.astype(BF16)


def _mla_attention(mqT, qn, mk, mvT, kn, ctx_kv, tq, tk):
    nq = mqT.shape[1]
    nkeys = mk.shape[0]
    has_ctx = ctx_kv is not None
    in_specs = [
        pl.BlockSpec((MLA_QPAD, tq), lambda h, i: (h, i)),
        pl.BlockSpec((8, tq), lambda h, i: (h, i)),
        pl.BlockSpec((nkeys, MLA_QPAD), lambda h, i: (0, h)),
        pl.BlockSpec((MLA_V, nkeys), lambda h, i: (h, 0)),
        pl.BlockSpec((8, nkeys), lambda h, i: (h, 0)),
    ]
    args = [mqT, qn, mk, mvT, kn]
    if has_ctx:
        kc, vcT, knc = ctx_kv
        nctx = kc.shape[0]
        in_specs += [
            pl.BlockSpec((nctx, MLA_QPAD), lambda h, i: (0, h)),
            pl.BlockSpec((MLA_V, nctx), lambda h, i: (h, 0)),
            pl.BlockSpec((8, nctx), lambda h, i: (h, 0)),
        ]
        args += [kc, vcT, knc]
    return pl.pallas_call(
        functools.partial(_mla_kernel, tk=tk, has_ctx=has_ctx),
        grid=(MLA_HEADS, nq // tq),
        in_specs=in_specs,
        out_specs=pl.BlockSpec((tq, MLA_V), lambda h, i: (i, h)),
        out_shape=jax.ShapeDtypeStruct((nq, MLA_HEADS * MLA_V), BF16),
        scratch_shapes=[pltpu.VMEM((1, tq), F32), pltpu.VMEM((1, tq), F32), pltpu.VMEM((MLA_V, tq), F32),
                        pltpu.VMEM((1, tq), F32),
                        pltpu.VMEM((tk, tq), F32), pltpu.VMEM((tk, tq), F32),
                        pltpu.VMEM((tk, tq), BF16), pltpu.VMEM((tk, tq), BF16),
                        pltpu.VMEM((MLA_QPAD, tq), BF16)],
        compiler_params=_cparams(("parallel", "arbitrary")),
        name="mla_attention",
    )(*args)


def _conv3_rows(x, prev_row, next_row, w):
    tm = x.shape[0]
    rows = lax.broadcasted_iota(jnp.int32, (tm, 1), 0)
    xm1 = jnp.where(rows == 0, prev_row, pltpu.roll(x, 1, 0))
    xp1 = jnp.where(rows == tm - 1, next_row, pltpu.roll(x, tm - 1, 0))
    return w[0:1, :] * xm1 + w[1:2, :] * x + w[2:3, :] * xp1


HALO = 16


def _merge_kernel(ya_ref, yb_ref, ubc_ref, prev_ref, next_ref, g0_ref, g1g_ref, g2_ref, g3_ref, x_ref, g1_ref,
                  cw_ref, wa_ref, wb_ref, wc_ref, wo_ref, o_ref, *, row):
    i = pl.program_id(0)
    nblk = pl.num_programs(0)

    def conv_windows(ref, rows):
        blk = ref[...].astype(F32)[rows]
        return [blk[:, g * CONV_CH:g * CONV_CH + CONV_W] for g in range(3)]

    u, gb, gc = conv_windows(ubc_ref, slice(None))
    pu, _, pc = conv_windows(prev_ref, slice(HALO - 1, HALO))
    nu, _, nc = conv_windows(next_ref, slice(0, 1))
    cu_prev = pc * pu * (i > 0).astype(F32)
    cu_next = nc * nu * (i < nblk - 1).astype(F32)
    yc = gb * _conv3_rows(gc * u, cu_prev, cu_next, cw_ref[...])
    lane = lax.broadcasted_iota(jnp.int32, yc.shape, 1)
    yc = jnp.where((lane >= OFF) & (lane < OFF + CONV_CH), yc, 0.0).astype(BF16)

    a = jnp.dot(ya_ref[...], wa_ref[...], preferred_element_type=F32)
    b = jnp.dot(yb_ref[...], wb_ref[...], preferred_element_type=F32)
    c = jnp.dot(yc, wc_ref[...], preferred_element_type=F32)

    def sig(ref, cols=slice(None)):
        return jax.nn.sigmoid(ref[:, cols].astype(F32))

    merged = sig(g0_ref) * a + sig(g1g_ref) * b + sig(g2_ref) * c
    first = slice(0, LANE)
    wrapped = sig(g1g_ref, first) * a[:, first] + sig(g2_ref, first) * b[:, first] + sig(g3_ref) * c[:, first]
    lane = lax.broadcasted_iota(jnp.int32, wrapped.shape, 1)
    head = jnp.where(lane < OFF, wrapped, merged[:, first])
    merged = jnp.concatenate([head, merged[:, LANE:]], axis=1)
    mix = jnp.dot(merged.astype(BF16), wo_ref[...], preferred_element_type=F32)
    o_ref[...] = x_ref[...] + g1_ref[row:row + 1, :] * mix


def _merge(ya, yb, pg, x2d, mod, g1_blk, row, conv_w8, wa, wb, wc, wo, layer, tm):
    n, d = x2d.shape
    ubc_blk = CONV_WIN0 // CONV_WIN
    gblk = GATE_WIN0 // d
    hb = tm // HALO
    last = n // HALO - 1
    return pl.pallas_call(
        functools.partial(_merge_kernel, row=row),
        grid=(n // tm,),
        in_specs=[
            pl.BlockSpec((tm, NA_W), lambda i: (i, 0)),
            pl.BlockSpec((tm, MLA_HEADS * MLA_V), lambda i: (i, 0)),
            pl.BlockSpec((tm, CONV_WIN), lambda i: (i, ubc_blk)),
            pl.BlockSpec((HALO, CONV_WIN), lambda i: (jnp.maximum(i * hb - 1, 0), ubc_blk)),
            pl.BlockSpec((HALO, CONV_WIN), lambda i: (jnp.minimum((i + 1) * hb, last), ubc_blk)),
            pl.BlockSpec((tm, d), lambda i: (i, gblk)),
            pl.BlockSpec((tm, d), lambda i: (i, gblk + 1)),
            pl.BlockSpec((tm, d), lambda i: (i, gblk + 2)),
            pl.BlockSpec((tm, LANE), lambda i: (i, (GATE_WIN0 + G_COLS) // LANE)),
            pl.BlockSpec((tm, d), lambda i: (i, 0)),
            pl.BlockSpec((8, d), lambda i: (0, g1_blk), pipeline_mode=pl.Buffered(1)),
            _layer_spec((8, CONV_W), layer),
            _layer_spec(wa.shape[1:], layer),
            _layer_spec(wb.shape[1:], layer),
            _layer_spec(wc.shape[1:], layer),
            _layer_spec(wo.shape[1:], layer),
        ],
        out_specs=pl.BlockSpec((tm, d), lambda i: (i, 0)),
        out_shape=jax.ShapeDtypeStruct((n, d), F32),
        compiler_params=_cparams(("parallel",)),
        name="merge_out",
    )(ya, yb, pg, pg, pg, pg, pg, pg, pg, x2d, mod, conv_w8, wa, wb, wc, wo)


FFN_KC = 256


def _ffn_down_kernel(a_ref, prev_ref, next_ref, val_ref, cw_ref, wd_ref, x_ref, g2_ref, o_ref,
                     acc_ref, h_a, h_b, *, row):
    i = pl.program_id(0)
    nblk = pl.num_programs(0)
    has_prev = (i > 0).astype(F32)
    has_next = (i < nblk - 1).astype(F32)
    nchunks = D_FF // FFN_KC

    def hidden(c, h_ref):
        sl = slice(c * FFN_KC, (c + 1) * FFN_KC)
        w = cw_ref[:, sl]
        a = a_ref[:, sl].astype(F32)
        pr = prev_ref[:, sl].astype(F32)[HALO - 1:HALO, :] * has_prev
        nx = next_ref[:, sl].astype(F32)[0:1, :] * has_next
        a = _conv3_rows(a, pr, nx, w) + w[3:4, :]
        h_ref[...] = (a * jax.nn.sigmoid(a) * val_ref[:, sl].astype(F32)).astype(BF16)

    bufs = (h_a, h_b)
    hidden(0, h_a)
    for c in range(nchunks):
        if c + 1 < nchunks:
            hidden(c + 1, bufs[(c + 1) % 2])
        part = jnp.dot(bufs[c % 2][...], wd_ref[c * FFN_KC:(c + 1) * FFN_KC, :], preferred_element_type=F32)
        if c == 0:
            acc_ref[...] = part
        else:
            acc_ref[...] += part
    o_ref[...] = x_ref[...] + g2_ref[row:row + 1, :] * acc_ref[...]


def _ffn_down(av, x2d, mod, g2_blk, row, cw8, wd, layer, tm):
    n, d = x2d.shape
    hb = tm // HALO
    last = n // HALO - 1
    return pl.pallas_call(
        functools.partial(_ffn_down_kernel, row=row),
        grid=(n // tm,),
        in_specs=[
            pl.BlockSpec((tm, D_FF), lambda i: (i, 0)),
            pl.BlockSpec((HALO, D_FF), lambda i: (jnp.maximum(i * hb - 1, 0), 0)),
            pl.BlockSpec((HALO, D_FF), lambda i: (jnp.minimum((i + 1) * hb, last), 0)),
            pl.BlockSpec((tm, D_FF), lambda i: (i, 1)),
            _layer_spec((8, D_FF), layer),
            _layer_spec((D_FF, d), layer),
            pl.BlockSpec((tm, d), lambda i: (i, 0)),
            pl.BlockSpec((8, d), lambda i: (0, g2_blk), pipeline_mode=pl.Buffered(1)),
        ],
        out_specs=pl.BlockSpec((tm, d), lambda i: (i, 0)),
        out_shape=jax.ShapeDtypeStruct((n, d), F32),
        scratch_shapes=[pltpu.VMEM((tm, d), F32), pltpu.VMEM((tm, FFN_KC), BF16), pltpu.VMEM((tm, FFN_KC), BF16)],
        compiler_params=_cparams(("parallel",)),
        name="ffn_down",
    )(av, av, av, av, cw8, wd, x2d, mod)


def _cast_pad_kernel(w_ref, o_ref):
    ncol = w_ref.shape[1]
    full = ncol // LANE * LANE
    o_ref[:, :full] = w_ref[:, :full].astype(BF16)
    o_ref[:, full:] = jnp.zeros((o_ref.shape[0], o_ref.shape[1] - full), BF16)
    o_ref[:, full:ncol] = w_ref[:, full:ncol].astype(BF16)


def _cast_pad_w_in(w_in):
    depth, d, ncol = w_in.shape
    tr = 256
    return pl.pallas_call(
        _cast_pad_kernel,
        grid=(depth, d // tr),
        in_specs=[pl.BlockSpec((None, tr, ncol), lambda l, i: (l, i, 0))],
        out_specs=pl.BlockSpec((None, tr, PG_COLS), lambda l, i: (l, i, 0)),
        out_shape=jax.ShapeDtypeStruct((depth, d, PG_COLS), BF16),
        compiler_params=_cparams(("parallel", "parallel")),
        name="cast_w_in",
    )(w_in)


def _pad_rows(a, rows):
    return jnp.concatenate([a, jnp.zeros((a.shape[0], rows - a.shape[1], a.shape[2]), a.dtype)], axis=1)


def _rope_tables_T(n_tok):
    t = jnp.arange(n_tok)
    row = (t // GRID_W).astype(F32)
    col = (t % GRID_W).astype(F32)
    inv = ROPE_BASE ** (-jnp.arange(ROPE_NFREQ, dtype=F32) / ROPE_NFREQ)
    ang = jnp.stack([inv[:, None] * row[None, :], inv[:, None] * col[None, :]], axis=0)
    return jnp.cos(ang), jnp.sin(ang)


def kernel(x, c, ctx, c_ctx, w_mod, b_mod, norm1_g, w_in, na_q_g, na_k_g, na_rel_bias, mla_kv_g, w_kv_up,
           mla_q_g, mla_k_g, conv_w, w_br_a, w_br_b, w_br_c, w_o, norm2_g, w_up, ffn_conv_w, ffn_conv_b, w_down):
    bsz, seq, d = x.shape
    nctx = ctx.shape[1]
    depth = w_mod.shape[0]
    rows = seq // GRID_W
    assert bsz == 1 and d == D_MODEL and seq % NA_TQ == 0 and rows >= NA_KROWS and nctx % 256 == 0

    xs = x[0]
    cs = ctx[0]
    tm_x = min(1024, seq)
    tm_half = min(512, seq)
    tm_merge = min(256, seq)
    tq_mla = min(1024, seq)
    tk_mla = min(1024, seq)

    cos_x, sin_x = _rope_tables_T(seq)
    cos_c = jnp.ones((2, ROPE_NFREQ, nctx), F32)
    sin_c = jnp.zeros((2, ROPE_NFREQ, nctx), F32)
    c_cols = jnp.zeros((d, 128), F32).at[:, 0].set(c[0]).at[:, 1].set(c_ctx)

    w_in_all = _cast_pad_w_in(w_in)
    kv_up = w_kv_up.reshape(depth, MLA_KV_RANK, MLA_HEADS, MLA_NOPE + MLA_V)
    wknT = jnp.swapaxes(kv_up[..., :MLA_NOPE].reshape(depth, MLA_KV_RANK, -1), 1, 2).astype(BF16)
    wvT = jnp.swapaxes(kv_up[..., MLA_NOPE:].reshape(depth, MLA_KV_RANK, -1), 1, 2).astype(BF16)
    gains = (na_q_g[..., None], na_k_g[..., None], mla_kv_g[..., None],
             _pad_rows(mla_q_g[..., None], MLA_QPAD), _pad_rows(mla_k_g[..., None], MLA_QPAD))
    bias = _na_bias_tables(na_rel_bias, rows)
    conv_w8 = jnp.pad(_pad_rows(conv_w, 8), ((0, 0), (0, 0), (OFF, OFF)))
    ffn_cw8 = _pad_rows(jnp.concatenate([ffn_conv_w, ffn_conv_b[:, None, :]], axis=1), 8)
    wa, wb, wc = (jnp.roll(w, OFF, axis=2).astype(BF16) for w in (w_br_a, w_br_b, w_br_c))
    wc = jnp.pad(wc, ((0, 0), (OFF, OFF), (0, 0)))
    wo = jnp.roll(w_o, OFF, axis=1).astype(BF16)
    w_up_all, w_down_all = w_up.astype(BF16), w_down.astype(BF16)
    n1g = norm1_g[:, None, :]
    n2g = norm2_g[:, None, :]

    for l in range(depth):
        last = l == depth - 1
        mod = _modulation(c_cols, w_mod, b_mod, l)

        pgc = _norm_matmul(cs, n1g, mod, 0, 1, w_in_all, l, nctx, IN_TN, "in_proj")
        naqcT, nakc, navcT, mqcT, mkc, mvcT, qnc, knc = _prep(pgc, cos_c, sin_c, gains, wknT, wvT, l, nctx)
        pgx = _norm_matmul(xs, n1g, mod, 0, 0, w_in_all, l, tm_x, IN_TN, "in_proj")
        naqT, nak, navT, mqT, mk, mvT, qn, kn = _prep(pgx, cos_x, sin_x, gains, wknT, wvT, l, tm_half)

        ya = _na_latent(naqT, nak, navT, nakc, navcT, bias, l)
        yb = _mla_attention(mqT, qn, mk, mvT, kn, (mkc, mvcT, knc), tq_mla, tk_mla)
        xs = _merge(ya, yb, pgx, xs, mod, 2, 0, conv_w8, wa, wb, wc, wo, l, tm_merge)
        av = _norm_matmul(xs, n2g, mod, 3, 0, w_up_all, l, tm_x, UP_TN, "ffn_up")
        xs = _ffn_down(av, xs, mod, 5, 0, ffn_cw8, w_down_all, l, tm_merge)

        if not last:
            yac = _na_ctx(naqcT, nakc, navcT)
            ybc = _mla_attention(mqcT, qnc, mkc, mvcT, knc, None, nctx, nctx)
            cs = _merge(yac, ybc, pgc, cs, mod, 2, 1, conv_w8, wa, wb, wc, wo, l, nctx)
            avc = _norm_matmul(cs, n2g, mod, 3, 1, w_up_all, l, nctx, UP_TN, "ffn_up")
            cs = _ffn_down(avc, cs, mod, 5, 1, ffn_cw8, w_down_all, l, nctx)

    return xs[None]
```

```python
import functools

import math

import jax
import jax.numpy as jnp
import numpy as np
from jax import lax
from jax.experimental import pallas as pl
from jax.experimental.pallas import tpu as pltpu

F32 = jnp.float32
BF16 = jnp.bfloat16

D_MODEL = 2048
GRID_W = 64
EPS = 1e-6
NA_HEADS = 8
NA_HEAD_DIM = 64
NA_WIN_H = 8
NA_WIN_W = 16
NA_W = NA_HEADS * NA_HEAD_DIM
MLA_HEADS = 4
MLA_NOPE = 128
MLA_ROPE = 64
MLA_V = 128
MLA_KV_RANK = 256
MLA_QK = MLA_NOPE + MLA_ROPE
MLA_QPAD = 256
ROPE_NFREQ = MLA_ROPE // 4
ROPE_BASE = 10000.0
CONV_CH = 512
D_FF = 5632

R_NA_K = 0
R_NA_V = R_NA_K + NA_W
R_CKV = R_NA_V + NA_W
R_KROPE = R_CKV + MLA_KV_RANK
R_NA_Q = R_KROPE + MLA_ROPE
R_MLA_Q = R_NA_Q + NA_W
R_CONV_U = R_MLA_Q + MLA_HEADS * MLA_QK
R_GATE = R_CONV_U + 3 * CONV_CH

LANE = 128
OFF = R_NA_Q % LANE
PG_COLS = 10752
QKV_WIN = 3072
NAQ_WIN = (R_NA_Q - OFF, R_MLA_Q + OFF)
MQ_WIN = (R_MLA_Q - OFF, R_CONV_U + OFF)
CONV_WIN0 = R_CONV_U - OFF
CONV_WIN = 2560
CONV_W = CONV_CH + 2 * OFF
GATE_WIN0 = R_GATE - OFF
G_COLS = 3 * D_MODEL

NA_QROWS = 4
NA_KROWS = 12
NA_TQ = NA_QROWS * GRID_W
NA_TK = NA_KROWS * GRID_W
NEG = -1e30
LOG2E = math.log2(math.e)
MLA_BOUND_MARGIN = 1.02
MLA_MAX_SHIFT = 50.0

VMEM_LIMIT = 56 * 1024 * 1024


def _cparams(sem):
    return pltpu.CompilerParams(dimension_semantics=sem, vmem_limit_bytes=VMEM_LIMIT)


def _const_spec(shape):
    nd = len(shape)
    return pl.BlockSpec(shape, lambda *_: (0,) * nd, pipeline_mode=pl.Buffered(1))


def _layer_spec(shape, layer):
    nd = len(shape)
    return pl.BlockSpec((None,) + tuple(shape), lambda *_: (layer,) + (0,) * nd, pipeline_mode=pl.Buffered(1))


MOD_KC = 256


def _mod_kernel(c_ref, w_ref, b_ref, o_ref):
    tn = o_ref.shape[1]
    nk = w_ref.shape[0] // MOD_KC

    def body(kc, acc):
        k0 = pl.multiple_of(kc * MOD_KC, MOD_KC)
        w = w_ref[pl.ds(k0, MOD_KC), :]
        c = c_ref[pl.ds(k0, MOD_KC), :]
        s = c * jax.nn.sigmoid(c)
        out = []
        for r in range(2):
            prod = s[:, r:r + 1] * w
            out.append(acc[r] + jnp.sum(prod.reshape(MOD_KC // 8, 8, tn), axis=0))
        return tuple(out)

    z = jnp.zeros((8, tn), F32)
    acc = lax.fori_loop(0, nk, body, (z, z))
    r0, r1 = [jnp.sum(a, axis=0, keepdims=True) + b_ref[...] for a in acc]
    ridx = lax.broadcasted_iota(jnp.int32, (8, tn), 0)
    o_ref[...] = jnp.where(ridx == 0, r0, jnp.where(ridx == 1, r1, 0.0))


def _modulation(c_cols, w_mod, b_mod, layer):
    depth, d, n = w_mod.shape
    tn = 1536
    return pl.pallas_call(
        _mod_kernel,
        grid=(n // tn,),
        in_specs=[
            _const_spec((d, 128)),
            pl.BlockSpec((None, d, tn), lambda j: (layer, 0, j)),
            pl.BlockSpec((None, 1, tn), lambda j: (layer, 0, j)),
        ],
        out_specs=pl.BlockSpec((8, tn), lambda j: (0, j)),
        out_shape=jax.ShapeDtypeStruct((8, n), F32),
        compiler_params=_cparams(("arbitrary",)),
        name="modulation",
    )(c_cols, w_mod, b_mod.reshape(depth, 1, n))


NORM_RC = 128
IN_TN = 1536
UP_TN = 1024


def _norm_rows(x_ref, g_ref, sh_ref, sc_ref, hx_ref, row):
    tm = x_ref.shape[0]
    g = g_ref[...]
    sc1 = 1.0 + sc_ref[row:row + 1, :]
    sh = sh_ref[row:row + 1, :]

    def body(r, carry):
        r0 = pl.multiple_of(r * NORM_RC, NORM_RC)
        x = x_ref[pl.ds(r0, NORM_RC), :]
        ms = jnp.mean(x * x, axis=-1, keepdims=True)
        y = (x * lax.rsqrt(ms + EPS)) * g
        hx_ref[pl.ds(r0, NORM_RC), :] = (y * sc1 + sh).astype(BF16)
        return carry

    lax.fori_loop(0, tm // NORM_RC, body, 0)


def _norm_matmul_kernel(x_ref, g_ref, sh_ref, sc_ref, w_ref, o_ref, hx_ref, *, row, w_transposed):
    @pl.when(pl.program_id(1) == 0)
    def _():
        _norm_rows(x_ref, g_ref, sh_ref, sc_ref, hx_ref, row)

    contract = (((1,), (1 if w_transposed else 0,)), ((), ()))
    o_ref[...] = lax.dot_general(hx_ref[...], w_ref[...], contract, preferred_element_type=F32).astype(BF16)


def _norm_matmul(x2d, norm_g, mod, sh_blk, row, w_all, layer, tm, tn, name, w_transposed=False):
    n, d = x2d.shape
    if w_transposed:
        ncol = w_all.shape[1]
        w_spec = pl.BlockSpec((None, tn, d), lambda i, j: (layer, j, 0))
    else:
        ncol = w_all.shape[2]
        w_spec = pl.BlockSpec((None, d, tn), lambda i, j: (layer, 0, j))
    return pl.pallas_call(
        functools.partial(_norm_matmul_kernel, row=row, w_transposed=w_transposed),
        grid=(n // tm, ncol // tn),
        in_specs=[
            pl.BlockSpec((tm, d), lambda i, j: (i, 0)),
            _layer_spec((1, d), layer),
            pl.BlockSpec((8, d), lambda i, j: (0, sh_blk), pipeline_mode=pl.Buffered(1)),
            pl.BlockSpec((8, d), lambda i, j: (0, sh_blk + 1), pipeline_mode=pl.Buffered(1)),
            w_spec,
        ],
        out_specs=pl.BlockSpec((tm, tn), lambda i, j: (i, j)),
        out_shape=jax.ShapeDtypeStruct((n, ncol), BF16),
        scratch_shapes=[pltpu.VMEM((tm, d), BF16)],
        compiler_params=_cparams(("parallel", "arbitrary")),
        name=name,
    )(x2d, norm_g, mod, mod, w_all)


def _rope_rows(r, cos_ref, sin_ref):
    out = []
    for part in range(2):
        cs = cos_ref[part]
        sn = sin_ref[part]
        x1 = r[32 * part:32 * part + 16]
        x2 = r[32 * part + 16:32 * part + 32]
        out.append(x1 * cs - x2 * sn)
        out.append(x2 * cs + x1 * sn)
    return out


def _prep_kernel(p_ref, cos_ref, sin_ref, naqg_ref, nakg_ref, kvg_ref, mqg_ref, mkg_ref,
                 wkn_ref, wv_ref,
                 naqT_ref, nak_ref, navT_ref, mqT_ref, mk_ref, mvT_ref, qn_ref, kn_ref):
    tm = p_ref.shape[0]

    def head_norm_T(xt, g_ref, scale):
        x3 = xt.reshape(NA_HEADS, NA_HEAD_DIM, tm)
        ms = jnp.mean(x3 * x3, axis=1, keepdims=True)
        y = x3 * lax.rsqrt(ms + EPS) * g_ref[...][None]
        if scale != 1.0:
            y = y * scale
        return y.reshape(NA_W, tm)

    naq_t = p_ref[:, NAQ_WIN[0]:NAQ_WIN[1]].astype(F32).T[OFF:OFF + NA_W]
    naqT_ref[...] = head_norm_T(naq_t, naqg_ref, NA_HEAD_DIM ** -0.5 * LOG2E).astype(BF16)
    nak_t = p_ref[:, R_NA_K:R_NA_V].astype(F32).T
    nak_ref[...] = head_norm_T(nak_t, nakg_ref, 1.0).T.astype(BF16)
    navT_ref[...] = p_ref[:, R_NA_V:R_CKV].astype(F32).T.astype(BF16)

    mqg = mqg_ref[...][:MLA_QK]
    qscale = MLA_QK ** -0.5 * LOG2E
    mq_t = p_ref[:, MQ_WIN[0]:MQ_WIN[1]].astype(F32).T
    for h in range(MLA_HEADS):
        qt = mq_t[OFF + h * MLA_QK:OFF + (h + 1) * MLA_QK]
        ms = jnp.sum(qt * qt, axis=0, keepdims=True) * (1.0 / MLA_QK)
        y = qt * lax.rsqrt(ms + EPS) * mqg
        r0 = h * MLA_QPAD
        mqT_ref[r0:r0 + MLA_NOPE, :] = (y[:MLA_NOPE] * qscale).astype(BF16)
        parts = _rope_rows(y[MLA_NOPE:], cos_ref, sin_ref)
        for k, part in enumerate(parts):
            a = r0 + MLA_NOPE + 16 * k
            mqT_ref[a:a + 16, :] = (part * qscale).astype(BF16)
        mqT_ref[r0 + MLA_QK:r0 + MLA_QPAD, :] = jnp.zeros((MLA_QPAD - MLA_QK, tm), BF16)
        qnorm = jnp.sqrt(jnp.sum(y * y, axis=0, keepdims=True)) * qscale
        qn_ref[8 * h:8 * h + 8, :] = jnp.broadcast_to(qnorm, (8, tm))

    ct = p_ref[:, R_CKV:R_KROPE].astype(F32).T
    ms = jnp.mean(ct * ct, axis=0, keepdims=True)
    cn = (ct * lax.rsqrt(ms + EPS) * kvg_ref[...]).astype(BF16)
    mvT_ref[...] = jnp.dot(wv_ref[...], cn, preferred_element_type=F32).astype(BF16)
    knT = jnp.dot(wkn_ref[...], cn, preferred_element_type=F32)
    krT = p_ref[:, R_KROPE:R_KROPE + LANE].astype(F32).T[:MLA_ROPE]
    kr_ss = jnp.sum(krT * krT, axis=0, keepdims=True)
    mkg = mkg_ref[...]
    pad = (lax.broadcasted_iota(jnp.int32, (MLA_QPAD - MLA_QK, tm), 0) == 0).astype(F32)
    for h in range(MLA_HEADS):
        kn = knT[h * MLA_NOPE:(h + 1) * MLA_NOPE]
        ms = (jnp.sum(kn * kn, axis=0, keepdims=True) + kr_ss) * (1.0 / MLA_QK)
        rinv = lax.rsqrt(ms + EPS)
        kn = kn * rinv * mkg[:MLA_NOPE]
        kr = krT * rinv * mkg[MLA_NOPE:MLA_QK]
        knorm = jnp.sqrt(jnp.sum(kn * kn, axis=0, keepdims=True) + jnp.sum(kr * kr, axis=0, keepdims=True))
        kn_ref[8 * h:8 * h + 8, :] = jnp.broadcast_to(knorm, (8, tm))
        pieces = [kn] + _rope_rows(kr, cos_ref, sin_ref) + [pad]
        kT = jnp.concatenate(pieces, axis=0)
        mk_ref[:, h * MLA_QPAD:(h + 1) * MLA_QPAD] = kT.T.astype(BF16)


def _prep(pg, cosT, sinT, gains, wknT, wvT, layer, tm):
    n = pg.shape[0]
    naqg, nakg, kvg, mqg, mkg = gains
    return pl.pallas_call(
        _prep_kernel,
        grid=(n // tm,),
        in_specs=[
            pl.BlockSpec((tm, QKV_WIN), lambda i: (i, 0)),
            pl.BlockSpec((2, ROPE_NFREQ, tm), lambda i: (0, 0, i)),
            pl.BlockSpec((2, ROPE_NFREQ, tm), lambda i: (0, 0, i)),
            _layer_spec((NA_HEAD_DIM, 1), layer),
            _layer_spec((NA_HEAD_DIM, 1), layer),
            _layer_spec((MLA_KV_RANK, 1), layer),
            _layer_spec((MLA_QPAD, 1), layer),
            _layer_spec((MLA_QPAD, 1), layer),
            _layer_spec((MLA_HEADS * MLA_NOPE, MLA_KV_RANK), layer),
            _layer_spec((MLA_HEADS * MLA_V, MLA_KV_RANK), layer),
        ],
        out_specs=[
            pl.BlockSpec((NA_W, tm), lambda i: (0, i)),
            pl.BlockSpec((tm, NA_W), lambda i: (i, 0)),
            pl.BlockSpec((NA_W, tm), lambda i: (0, i)),
            pl.BlockSpec((MLA_HEADS * MLA_QPAD, tm), lambda i: (0, i)),
            pl.BlockSpec((tm, MLA_HEADS * MLA_QPAD), lambda i: (i, 0)),
            pl.BlockSpec((MLA_HEADS * MLA_V, tm), lambda i: (0, i)),
            pl.BlockSpec((MLA_HEADS * 8, tm), lambda i: (0, i)),
            pl.BlockSpec((MLA_HEADS * 8, tm), lambda i: (0, i)),
        ],
        out_shape=[
            jax.ShapeDtypeStruct((NA_W, n), BF16),
            jax.ShapeDtypeStruct((n, NA_W), BF16),
            jax.ShapeDtypeStruct((NA_W, n), BF16),
            jax.ShapeDtypeStruct((MLA_HEADS * MLA_QPAD, n), BF16),
            jax.ShapeDtypeStruct((n, MLA_HEADS * MLA_QPAD), BF16),
            jax.ShapeDtypeStruct((MLA_HEADS * MLA_V, n), BF16),
            jax.ShapeDtypeStruct((MLA_HEADS * 8, n), F32),
            jax.ShapeDtypeStruct((MLA_HEADS * 8, n), F32),
        ],
        compiler_params=_cparams(("parallel",)),
        name="qkv_prep",
    )(pg, cosT, sinT, naqg, nakg, kvg, mqg, mkg, wknT, wvT)


def _na_heads(q_ref, k_refs, vT_refs, bias_ref, o_ref):
    tq = q_ref.shape[1]
    upper = lax.broadcasted_iota(jnp.int32, (128, tq), 0) >= NA_HEAD_DIM
    outs = []
    for h in range(NA_HEADS):
        j, e = divmod(h, 2)
        q2 = q_ref[128 * j:128 * j + 128, :]
        qm = jnp.where(upper == bool(e), q2, jnp.zeros_like(q2))
        s_list = []
        off = 0
        for b, k_ref in enumerate(k_refs):
            s = jnp.dot(k_ref[:, 128 * j:128 * j + 128], qm, preferred_element_type=F32)
            nk = k_ref.shape[0]
            if bias_ref is not None and b < len(k_refs) - 1:
                s = s + bias_ref[0, h, off:off + nk, :]
            off += nk
            s_list.append(s)
        m = s_list[0].max(axis=0, keepdims=True)
        for s in s_list[1:]:
            m = jnp.maximum(m, s.max(axis=0, keepdims=True))
        l = jnp.zeros((1, tq), F32)
        acc = jnp.zeros((NA_HEAD_DIM, tq), F32)
        for s, vT_ref in zip(s_list, vT_refs):
            p = jnp.exp2(s - m)
            l = l + p.sum(axis=0, keepdims=True)
            acc = acc + jnp.dot(vT_ref[NA_HEAD_DIM * h:NA_HEAD_DIM * (h + 1), :], p.astype(BF16),
                                preferred_element_type=F32)
        outs.append(acc / l)
    o_ref[...] = jnp.concatenate(outs, axis=0).T.astype(BF16)


def _na_latent_kernel(q_ref, k0, k1, k2, kc, v0, v1, v2, vc, bias_ref, o_ref):
    _na_heads(q_ref, [k0, k1, k2, kc], [v0, v1, v2, vc], bias_ref, o_ref)


def _na_ctx_kernel(q_ref, kc, vc, o_ref):
    _na_heads(q_ref, [kc], [vc], None, o_ref)


def _na_latent(naqT, nak, navT, nakc, navcT, bias, layer):
    n = naqT.shape[1]
    nb = n // NA_TQ
    nctx = nakc.shape[0]

    def kb(o):
        return lambda i: (jnp.clip(i - 1, 0, nb - 3) + o, 0)

    def vb(o):
        return lambda i: (0, jnp.clip(i - 1, 0, nb - 3) + o)

    def variant(i):
        return (layer, jnp.where(i == 0, 0, jnp.where(i == nb - 1, 2, 1)), 0, 0, 0)

    return pl.pallas_call(
        _na_latent_kernel,
        grid=(nb,),
        in_specs=[
            pl.BlockSpec((NA_W, NA_TQ), lambda i: (0, i)),
            pl.BlockSpec((NA_TQ, NA_W), kb(0)),
            pl.BlockSpec((NA_TQ, NA_W), kb(1)),
            pl.BlockSpec((NA_TQ, NA_W), kb(2)),
            _const_spec((nctx, NA_W)),
            pl.BlockSpec((NA_W, NA_TQ), vb(0)),
            pl.BlockSpec((NA_W, NA_TQ), vb(1)),
            pl.BlockSpec((NA_W, NA_TQ), vb(2)),
            _const_spec((NA_W, nctx)),
            pl.BlockSpec((None, 1, NA_HEADS, NA_TK, NA_TQ), variant),
        ],
        out_specs=pl.BlockSpec((NA_TQ, NA_W), lambda i: (i, 0)),
        out_shape=jax.ShapeDtypeStruct((n, NA_W), BF16),
        compiler_params=_cparams(("parallel",)),
        name="na_latent",
    )(naqT, nak, nak, nak, nakc, navT, navT, navT, navcT, bias)


def _na_ctx(naqcT, nakc, navcT):
    nctx = nakc.shape[0]
    return pl.pallas_call(
        _na_ctx_kernel,
        grid=(1,),
        in_specs=[_const_spec((NA_W, nctx)), _const_spec((nctx, NA_W)), _const_spec((NA_W, nctx))],
        out_specs=pl.BlockSpec((nctx, NA_W), lambda i: (0, 0)),
        out_shape=jax.ShapeDtypeStruct((nctx, NA_W), BF16),
        compiler_params=_cparams(("arbitrary",)),
        name="na_ctx",
    )(naqcT, nakc, navcT)


def _na_bias_tables(rel_bias, rows):
    ndc = 2 * NA_WIN_W - 1
    col = np.arange(GRID_W)
    start_c = np.clip(col - NA_WIN_W // 2, 0, GRID_W - NA_WIN_W)
    in_col = (col[:, None] >= start_c[None, :]) & (col[:, None] < start_c[None, :] + NA_WIN_W)
    dc = col[:, None] - col[None, :] + (NA_WIN_W - 1)
    sel_c = (dc[:, :, None] == np.arange(ndc)).astype(np.float32)
    by_col = jnp.einsum('lhrc,kqc->lhrkq', rel_bias.astype(F32), sel_c, precision=lax.Precision.HIGHEST)
    by_col = jnp.where(in_col, by_col * LOG2E, NEG)
    masked = jnp.full(by_col.shape[:2] + (GRID_W, GRID_W), NEG, F32)
    tabs = []
    for r0, base in ((0, 0), (NA_QROWS, 0), (rows - NA_QROWS, rows - NA_KROWS)):
        key_rows = []
        for a in range(NA_KROWS):
            blocks = []
            for b in range(NA_QROWS):
                qr, kr = r0 + b, base + a
                start_r = min(max(qr - NA_WIN_H // 2, 0), rows - NA_WIN_H)
                inside = start_r <= kr < start_r + NA_WIN_H
                blocks.append(by_col[:, :, kr - qr + NA_WIN_H - 1] if inside else masked)
            key_rows.append(jnp.concatenate(blocks, axis=-1))
        tabs.append(jnp.concatenate(key_rows, axis=-2))
    return jnp.stack(tabs, axis=1)


def _mla_kernel(*refs, tk, has_ctx):
    if has_ctx:
        qT_ref, qn_ref, k_ref, vT_ref, kn_ref, kc_ref, vcT_ref, knc_ref, o_ref = refs[:9]
    else:
        qT_ref, qn_ref, k_ref, vT_ref, kn_ref, o_ref = refs[:6]
    m_ref, l_ref, acc_ref, al_ref, s_a, s_b, p_a, p_b, qs_ref = refs[-9:]
    tq = qT_ref.shape[1]
    nk = k_ref.shape[0] // tk

    kmax = jnp.max(kn_ref[0:1, :])
    if has_ctx:
        kmax = jnp.maximum(kmax, jnp.max(knc_ref[0:1, :]))
    bound = qn_ref[0:1, :] * (kmax * MLA_BOUND_MARGIN)
    shift_ok = jnp.max(bound) <= MLA_MAX_SHIFT

    @pl.when(shift_ok)
    def _():
        qs_ref[...] = qT_ref[...]
        row0 = lax.broadcasted_iota(jnp.int32, (16, tq), 0) == 0
        qs_ref[MLA_QK:MLA_QK + 16, :] = jnp.where(row0, -bound, 0.0).astype(BF16)

        def chunk(k_blk, vT_blk):
            p = jnp.exp2(jnp.dot(k_blk, qs_ref[...], preferred_element_type=F32))
            return p.sum(axis=0, keepdims=True), jnp.dot(vT_blk, p.astype(BF16), preferred_element_type=F32)

        if has_ctx:
            l0, a0 = chunk(kc_ref[...], vcT_ref[...])
            l_ref[...] = l0
            acc_ref[...] = a0
        else:
            l_ref[...] = jnp.zeros((1, tq), F32)
            acc_ref[...] = jnp.zeros((MLA_V, tq), F32)
        unroll = next(u for u in (8, 4, 2, 1) if nk % u == 0)

        def body(it, carry):
            l_sum = l_ref[...]
            a_sum = acc_ref[...]
            for u in range(unroll):
                k0 = pl.multiple_of((it * unroll + u) * tk, tk)
                lp, ap = chunk(k_ref[pl.ds(k0, tk), :], vT_ref[:, pl.ds(k0, tk)])
                l_sum = l_sum + lp
                a_sum = a_sum + ap
            l_ref[...] = l_sum
            acc_ref[...] = a_sum
            return carry

        lax.fori_loop(0, nk // unroll, body, 0)

    @pl.when(jnp.logical_not(shift_ok))
    def _():
        if has_ctx:
            s = jnp.dot(kc_ref[...], qT_ref[...], preferred_element_type=F32)
            m = s.max(axis=0, keepdims=True)
            p = jnp.exp2(s - m)
            m_ref[...] = m
            l_ref[...] = p.sum(axis=0, keepdims=True)
            acc_ref[...] = jnp.dot(vcT_ref[...], p.astype(BF16), preferred_element_type=F32)
        else:
            m_ref[...] = jnp.full((1, tq), NEG, F32)
            l_ref[...] = jnp.zeros((1, tq), F32)
            acc_ref[...] = jnp.zeros((MLA_V, tq), F32)

        def scores(chunk, s_ref):
            k0 = pl.multiple_of(chunk * tk, tk)
            s_ref[...] = jnp.dot(k_ref[pl.ds(k0, tk), :], qT_ref[...], preferred_element_type=F32)

        def softmax(s_ref, p_ref):
            s = s_ref[...]
            m_old = m_ref[...]
            m_new = jnp.maximum(m_old, s.max(axis=0, keepdims=True))
            alpha = jnp.exp2(m_old - m_new)
            p = jnp.exp2(s - m_new)
            l_ref[...] = alpha * l_ref[...] + p.sum(axis=0, keepdims=True)
            m_ref[...] = m_new
            p_ref[...] = p.astype(BF16)
            return alpha

        def values(chunk, p_ref, alpha):
            k0 = pl.multiple_of(chunk * tk, tk)
            acc_ref[...] = alpha * acc_ref[...] + jnp.dot(vT_ref[:, pl.ds(k0, tk)], p_ref[...],
                                                          preferred_element_type=F32)

        if nk % 2 == 0:
            scores(0, s_a)
            p_b[...] = jnp.zeros(p_b.shape, BF16)
            al_ref[...] = jnp.ones((1, tq), F32)

            def body(it, carry):
                c0 = 2 * it
                scores(c0 + 1, s_b)
                a0 = softmax(s_a, p_a)
                values(jnp.maximum(c0 - 1, 0), p_b, al_ref[...])
                scores(jnp.minimum(c0 + 2, nk - 1), s_a)
                a1 = softmax(s_b, p_b)
                values(c0, p_a, a0)
                al_ref[...] = a1
                return carry

            lax.fori_loop(0, nk // 2, body, 0)
            values(nk - 1, p_b, al_ref[...])
        else:
            for c in range(nk):
                scores(c, s_a)
                values(c, p_a, softmax(s_a, p_a))

    o_ref[...] = (acc_ref[...] / l_ref[...]).T.astype(BF16)


def _mla_attention(mqT, qn, mk, mvT, kn, ctx_kv, tq, tk):
    nq = mqT.shape[1]
    nkeys = mk.shape[0]
    has_ctx = ctx_kv is not None
    in_specs = [
        pl.BlockSpec((MLA_QPAD, tq), lambda h, i: (h, i)),
        pl.BlockSpec((8, tq), lambda h, i: (h, i)),
        pl.BlockSpec((nkeys, MLA_QPAD), lambda h, i: (0, h)),
        pl.BlockSpec((MLA_V, nkeys), lambda h, i: (h, 0)),
        pl.BlockSpec((8, nkeys), lambda h, i: (h, 0)),
    ]
    args = [mqT, qn, mk, mvT, kn]
    if has_ctx:
        kc, vcT, knc = ctx_kv
        nctx = kc.shape[0]
        in_specs += [
            pl.BlockSpec((nctx, MLA_QPAD), lambda h, i: (0, h)),
            pl.BlockSpec((MLA_V, nctx), lambda h, i: (h, 0)),
            pl.BlockSpec((8, nctx), lambda h, i: (h, 0)),
        ]
        args += [kc, vcT, knc]
    return pl.pallas_call(
        functools.partial(_mla_kernel, tk=tk, has_ctx=has_ctx),
        grid=(MLA_HEADS, nq // tq),
        in_specs=in_specs,
        out_specs=pl.BlockSpec((tq, MLA_V), lambda h, i: (i, h)),
        out_shape=jax.ShapeDtypeStruct((nq, MLA_HEADS * MLA_V), BF16),
        scratch_shapes=[pltpu.VMEM((1, tq), F32), pltpu.VMEM((1, tq), F32), pltpu.VMEM((MLA_V, tq), F32),
                        pltpu.VMEM((1, tq), F32),
                        pltpu.VMEM((tk, tq), F32), pltpu.VMEM((tk, tq), F32),
                        pltpu.VMEM((tk, tq), BF16), pltpu.VMEM((tk, tq), BF16),
                        pltpu.VMEM((MLA_QPAD, tq), BF16)],
        compiler_params=_cparams(("parallel", "arbitrary")),
        name="mla_attention",
    )(*args)


def _conv3_rows(x, prev_row, next_row, w):
    tm = x.shape[0]
    rows = lax.broadcasted_iota(jnp.int32, (tm, 1), 0)
    xm1 = jnp.where(rows == 0, prev_row, pltpu.roll(x, 1, 0))
    xp1 = jnp.where(rows == tm - 1, next_row, pltpu.roll(x, tm - 1, 0))
    return w[0:1, :] * xm1 + w[1:2, :] * x + w[2:3, :] * xp1


HALO = 16


def _merge_kernel(ya_ref, yb_ref, ubc_ref, prev_ref, next_ref, g0_ref, g1g_ref, g2_ref, g3_ref, x_ref, g1_ref,
                  cw_ref, wa_ref, wb_ref, wc_ref, wo_ref, o_ref, *, row):
    i = pl.program_id(0)
    nblk = pl.num_programs(0)

    def conv_windows(ref, rows):
        blk = ref[...].astype(F32)[rows]
        return [blk[:, g * CONV_CH:g * CONV_CH + CONV_W] for g in range(3)]

    u, gb, gc = conv_windows(ubc_ref, slice(None))
    pu, _, pc = conv_windows(prev_ref, slice(HALO - 1, HALO))
    nu, _, nc = conv_windows(next_ref, slice(0, 1))
    cu_prev = pc * pu * (i > 0).astype(F32)
    cu_next = nc * nu * (i < nblk - 1).astype(F32)
    yc = gb * _conv3_rows(gc * u, cu_prev, cu_next, cw_ref[...])
    lane = lax.broadcasted_iota(jnp.int32, yc.shape, 1)
    yc = jnp.where((lane >= OFF) & (lane < OFF + CONV_CH), yc, 0.0).astype(BF16)

    a = jnp.dot(ya_ref[...], wa_ref[...], preferred_element_type=F32)
    b = jnp.dot(yb_ref[...], wb_ref[...], preferred_element_type=F32)
    c = jnp.dot(yc, wc_ref[...], preferred_element_type=F32)

    def sig(ref, cols=slice(None)):
        return jax.nn.sigmoid(ref[:, cols].astype(F32))

    merged = sig(g0_ref) * a + sig(g1g_ref) * b + sig(g2_ref) * c
    first = slice(0, LANE)
    wrapped = sig(g1g_ref, first) * a[:, first] + sig(g2_ref, first) * b[:, first] + sig(g3_ref) * c[:, first]
    lane = lax.broadcasted_iota(jnp.int32, wrapped.shape, 1)
    head = jnp.where(lane < OFF, wrapped, merged[:, first])
    merged = jnp.concatenate([head, merged[:, LANE:]], axis=1)
    mix = jnp.dot(merged.astype(BF16), wo_ref[...], preferred_element_type=F32)
    o_ref[...] = x_ref[...] + g1_ref[row:row + 1, :] * mix


def _merge(ya, yb, pg, x2d, mod, g1_blk, row, conv_w8, wa, wb, wc, wo, layer, tm):
    n, d = x2d.shape
    ubc_blk = CONV_WIN0 // CONV_WIN
    gblk = GATE_WIN0 // d
    hb = tm // HALO
    last = n // HALO - 1
    return pl.pallas_call(
        functools.partial(_merge_kernel, row=row),
        grid=(n // tm,),
        in_specs=[
            pl.BlockSpec((tm, NA_W), lambda i: (i, 0)),
            pl.BlockSpec((tm, MLA_HEADS * MLA_V), lambda i: (i, 0)),
            pl.BlockSpec((tm, CONV_WIN), lambda i: (i, ubc_blk)),
            pl.BlockSpec((HALO, CONV_WIN), lambda i: (jnp.maximum(i * hb - 1, 0), ubc_blk)),
            pl.BlockSpec((HALO, CONV_WIN), lambda i: (jnp.minimum((i + 1) * hb, last), ubc_blk)),
            pl.BlockSpec((tm, d), lambda i: (i, gblk)),
            pl.BlockSpec((tm, d), lambda i: (i, gblk + 1)),
            pl.BlockSpec((tm, d), lambda i: (i, gblk + 2)),
            pl.BlockSpec((tm, LANE), lambda i: (i, (GATE_WIN0 + G_COLS) // LANE)),
            pl.BlockSpec((tm, d), lambda i: (i, 0)),
            pl.BlockSpec((8, d), lambda i: (0, g1_blk), pipeline_mode=pl.Buffered(1)),
            _layer_spec((8, CONV_W), layer),
            _layer_spec(wa.shape[1:], layer),
            _layer_spec(wb.shape[1:], layer),
            _layer_spec(wc.shape[1:], layer),
            _layer_spec(wo.shape[1:], layer),
        ],
        out_specs=pl.BlockSpec((tm, d), lambda i: (i, 0)),
        out_shape=jax.ShapeDtypeStruct((n, d), F32),
        compiler_params=_cparams(("parallel",)),
        name="merge_out",
    )(ya, yb, pg, pg, pg, pg, pg, pg, pg, x2d, mod, conv_w8, wa, wb, wc, wo)


FFN_KC = 256


def _ffn_down_kernel(a_ref, prev_ref, next_ref, val_ref, cw_ref, wd_ref, x_ref, g2_ref, o_ref,
                     acc_ref, h_a, h_b, *, row):
    i = pl.program_id(0)
    nblk = pl.num_programs(0)
    has_prev = (i > 0).astype(F32)
    has_next = (i < nblk - 1).astype(F32)
    nchunks = D_FF // FFN_KC

    def hidden(c, h_ref):
        sl = slice(c * FFN_KC, (c + 1) * FFN_KC)
        w = cw_ref[:, sl]
        a = a_ref[:, sl].astype(F32)
        pr = prev_ref[:, sl].astype(F32)[HALO - 1:HALO, :] * has_prev
        nx = next_ref[:, sl].astype(F32)[0:1, :] * has_next
        a = _conv3_rows(a, pr, nx, w) + w[3:4, :]
        h_ref[...] = (a * jax.nn.sigmoid(a) * val_ref[:, sl].astype(F32)).astype(BF16)

    bufs = (h_a, h_b)
    hidden(0, h_a)
    for c in range(nchunks):
        if c + 1 < nchunks:
            hidden(c + 1, bufs[(c + 1) % 2])
        part = jnp.dot(bufs[c % 2][...], wd_ref[c * FFN_KC:(c + 1) * FFN_KC, :], preferred_element_type=F32)
        if c == 0:
            acc_ref[...] = part
        else:
            acc_ref[...] += part
    o_ref[...] = x_ref[...] + g2_ref[row:row + 1, :] * acc_ref[...]


def _ffn_down(av, x2d, mod, g2_blk, row, cw8, wd, layer, tm):
    n, d = x2d.shape
    hb = tm // HALO
    last = n // HALO - 1
    return pl.pallas_call(
        functools.partial(_ffn_down_kernel, row=row),
        grid=(n // tm,),
        in_specs=[
            pl.BlockSpec((tm, D_FF), lambda i: (i, 0)),
            pl.BlockSpec((HALO, D_FF), lambda i: (jnp.maximum(i * hb - 1, 0), 0)),
            pl.BlockSpec((HALO, D_FF), lambda i: (jnp.minimum((i + 1) * hb, last), 0)),
            pl.BlockSpec((tm, D_FF), lambda i: (i, 1)),
            _layer_spec((8, D_FF), layer),
            _layer_spec((D_FF, d), layer),
            pl.BlockSpec((tm, d), lambda i: (i, 0)),
            pl.BlockSpec((8, d), lambda i: (0, g2_blk), pipeline_mode=pl.Buffered(1)),
        ],
        out_specs=pl.BlockSpec((tm, d), lambda i: (i, 0)),
        out_shape=jax.ShapeDtypeStruct((n, d), F32),
        scratch_shapes=[pltpu.VMEM((tm, d), F32), pltpu.VMEM((tm, FFN_KC), BF16), pltpu.VMEM((tm, FFN_KC), BF16)],
        compiler_params=_cparams(("parallel",)),
        name="ffn_down",
    )(av, av, av, av, cw8, wd, x2d, mod)


W_IN_TR = 448


def _cast_pad_kernel(w_ref, o_ref, *, n_real):
    w = w_ref[...].astype(BF16)
    o_ref[...] = jnp.where(pl.program_id(1) < n_real, w, jnp.zeros_like(w))


def _cast_pad_w_in(w_in):
    depth, d, ncol = w_in.shape
    assert ncol % W_IN_TR == 0 and PG_COLS % W_IN_TR == 0
    n_real = ncol // W_IN_TR
    return pl.pallas_call(
        functools.partial(_cast_pad_kernel, n_real=n_real),
        grid=(depth, PG_COLS // W_IN_TR),
        in_specs=[pl.BlockSpec((None, W_IN_TR, d), lambda l, i: (l, jnp.minimum(i, n_real - 1), 0))],
        out_specs=pl.BlockSpec((None, W_IN_TR, d), lambda l, i: (l, i, 0)),
        out_shape=jax.ShapeDtypeStruct((depth, PG_COLS, d), BF16),
        compiler_params=_cparams(("parallel", "parallel")),
        name="cast_w_in",
    )(jnp.swapaxes(w_in, 1, 2))


def _pad_rows(a, rows):
    return jnp.concatenate([a, jnp.zeros((a.shape[0], rows - a.shape[1], a.shape[2]), a.dtype)], axis=1)


def _rope_tables_T(n_tok):
    t = jnp.arange(n_tok)
    row = (t // GRID_W).astype(F32)
    col = (t % GRID_W).astype(F32)
    inv = ROPE_BASE ** (-jnp.arange(ROPE_NFREQ, dtype=F32) / ROPE_NFREQ)
    ang = jnp.stack([inv[:, None] * row[None, :], inv[:, None] * col[None, :]], axis=0)
    return jnp.cos(ang), jnp.sin(ang)


def kernel(x, c, ctx, c_ctx, w_mod, b_mod, norm1_g, w_in, na_q_g, na_k_g, na_rel_bias, mla_kv_g, w_kv_up,
           mla_q_g, mla_k_g, conv_w, w_br_a, w_br_b, w_br_c, w_o, norm2_g, w_up, ffn_conv_w, ffn_conv_b, w_down):
    bsz, seq, d = x.shape
    nctx = ctx.shape[1]
    depth = w_mod.shape[0]
    rows = seq // GRID_W
    assert bsz == 1 and d == D_MODEL and seq % NA_TQ == 0 and rows >= NA_KROWS and nctx % 256 == 0

    xs = x[0]
    cs = ctx[0]
    tm_x = min(1024, seq)
    tm_half = min(512, seq)
    tm_merge = min(256, seq)
    tq_mla = min(1024, seq)
    tk_mla = min(1024, seq)

    cos_x, sin_x = _rope_tables_T(seq)
    cos_c = jnp.ones((2, ROPE_NFREQ, nctx), F32)
    sin_c = jnp.zeros((2, ROPE_NFREQ, nctx), F32)
    c_cols = jnp.zeros((d, 128), F32).at[:, 0].set(c[0]).at[:, 1].set(c_ctx)

    w_in_all = _cast_pad_w_in(w_in)
    kv_up = w_kv_up.reshape(depth, MLA_KV_RANK, MLA_HEADS, MLA_NOPE + MLA_V)
    wknT = jnp.swapaxes(kv_up[..., :MLA_NOPE].reshape(depth, MLA_KV_RANK, -1), 1, 2).astype(BF16)
    wvT = jnp.swapaxes(kv_up[..., MLA_NOPE:].reshape(depth, MLA_KV_RANK, -1), 1, 2).astype(BF16)
    gains = (na_q_g[..., None], na_k_g[..., None], mla_kv_g[..., None],
             _pad_rows(mla_q_g[..., None], MLA_QPAD), _pad_rows(mla_k_g[..., None], MLA_QPAD))
    bias = _na_bias_tables(na_rel_bias, rows)
    conv_w8 = jnp.pad(_pad_rows(conv_w, 8), ((0, 0), (0, 0), (OFF, OFF)))
    ffn_cw8 = _pad_rows(jnp.concatenate([ffn_conv_w, ffn_conv_b[:, None, :]], axis=1), 8)
    wa, wb, wc = (jnp.roll(w, OFF, axis=2).astype(BF16) for w in (w_br_a, w_br_b, w_br_c))
    wc = jnp.pad(wc, ((0, 0), (OFF, OFF), (0, 0)))
    wo = jnp.roll(w_o, OFF, axis=1).astype(BF16)
    w_up_all, w_down_all = w_up.astype(BF16), w_down.astype(BF16)
    n1g = norm1_g[:, None, :]
    n2g = norm2_g[:, None, :]

    for l in range(depth):
        last = l == depth - 1
        mod = _modulation(c_cols, w_mod, b_mod, l)

        pgc = _norm_matmul(cs, n1g, mod, 0, 1, w_in_all, l, nctx, IN_TN, "in_proj", w_transposed=True)
        naqcT, nakc, navcT, mqcT, mkc, mvcT, qnc, knc = _prep(pgc, cos_c, sin_c, gains, wknT, wvT, l, nctx)
        pgx = _norm_matmul(xs, n1g, mod, 0, 0, w_in_all, l, tm_x, IN_TN, "in_proj", w_transposed=True)
        naqT, nak, navT, mqT, mk, mvT, qn, kn = _prep(pgx, cos_x, sin_x, gains, wknT, wvT, l, tm_half)

        ya = _na_latent(naqT, nak, navT, nakc, navcT, bias, l)
        yb = _mla_attention(mqT, qn, mk, mvT, kn, (mkc, mvcT, knc), tq_mla, tk_mla)
        xs = _merge(ya, yb, pgx, xs, mod, 2, 0, conv_w8, wa, wb, wc, wo, l, tm_merge)
        av = _norm_matmul(xs, n2g, mod, 3, 0, w_up_all, l, tm_x, UP_TN, "ffn_up")
        xs = _ffn_down(av, xs, mod, 5, 0, ffn_cw8, w_down_all, l, tm_merge)

        if not last:
            yac = _na_ctx(naqcT, nakc, navcT)
            ybc = _mla_attention(mqcT, qnc, mkc, mvcT, knc, None, nctx, nctx)
            cs = _merge(yac, ybc, pgc, cs, mod, 2, 1, conv_w8, wa, wb, wc, wo, l, nctx)
            avc = _norm_matmul(cs, n2g, mod, 3, 1, w_up_all, l, nctx, UP_TN, "ffn_up")
            cs = _ffn_down(avc, cs, mod, 5, 1, ffn_cw8, w_down_all, l, nctx)

    return xs[None]
```

```python
import functools

import math

import jax
import jax.numpy as jnp
import numpy as np
from jax import lax
from jax.experimental import pallas as pl
from jax.experimental.pallas import tpu as pltpu

F32 = jnp.float32
BF16 = jnp.bfloat16

D_MODEL = 2048
GRID_W = 64
EPS = 1e-6
NA_HEADS = 8
NA_HEAD_DIM = 64
NA_WIN_H = 8
NA_WIN_W = 16
NA_W = NA_HEADS * NA_HEAD_DIM
MLA_HEADS = 4
MLA_NOPE = 128
MLA_ROPE = 64
MLA_V = 128
MLA_KV_RANK = 256
MLA_QK = MLA_NOPE + MLA_ROPE
MLA_QPAD = 256
ROPE_NFREQ = MLA_ROPE // 4
ROPE_BASE = 10000.0
CONV_CH = 512
D_FF = 5632

R_NA_K = 0
R_NA_V = R_NA_K + NA_W
R_CKV = R_NA_V + NA_W
R_KROPE = R_CKV + MLA_KV_RANK
R_NA_Q = R_KROPE + MLA_ROPE
R_MLA_Q = R_NA_Q + NA_W
R_CONV_U = R_MLA_Q + MLA_HEADS * MLA_QK
R_GATE = R_CONV_U + 3 * CONV_CH

LANE = 128
OFF = R_NA_Q % LANE
PG_COLS = 10752
QKV_WIN = 3072
NAQ_WIN = (R_NA_Q - OFF, R_MLA_Q + OFF)
MQ_WIN = (R_MLA_Q - OFF, R_CONV_U + OFF)
CONV_WIN0 = R_CONV_U - OFF
CONV_WIN = 2560
CONV_W = CONV_CH + 2 * OFF
GATE_WIN0 = R_GATE - OFF
G_COLS = 3 * D_MODEL

NA_QROWS = 4
NA_KROWS = 12
NA_TQ = NA_QROWS * GRID_W
NA_TK = NA_KROWS * GRID_W
NEG = -1e30
LOG2E = math.log2(math.e)
MLA_BOUND_MARGIN = 1.02
MLA_MAX_SHIFT = 50.0

VMEM_LIMIT = 56 * 1024 * 1024


def _cparams(sem):
    return pltpu.CompilerParams(dimension_semantics=sem, vmem_limit_bytes=VMEM_LIMIT)


def _const_spec(shape):
    nd = len(shape)
    return pl.BlockSpec(shape, lambda *_: (0,) * nd, pipeline_mode=pl.Buffered(1))


def _layer_spec(shape, layer):
    nd = len(shape)
    return pl.BlockSpec((None,) + tuple(shape), lambda *_: (layer,) + (0,) * nd, pipeline_mode=pl.Buffered(1))


MOD_KC = 256


def _mod_kernel(c_ref, w_ref, b_ref, o_ref):
    tn = o_ref.shape[1]
    nk = w_ref.shape[0] // MOD_KC

    def body(kc, acc):
        k0 = pl.multiple_of(kc * MOD_KC, MOD_KC)
        w = w_ref[pl.ds(k0, MOD_KC), :]
        c = c_ref[pl.ds(k0, MOD_KC), :]
        s = c * jax.nn.sigmoid(c)
        out = []
        for r in range(2):
            prod = s[:, r:r + 1] * w
            out.append(acc[r] + jnp.sum(prod.reshape(MOD_KC // 8, 8, tn), axis=0))
        return tuple(out)

    z = jnp.zeros((8, tn), F32)
    acc = lax.fori_loop(0, nk, body, (z, z))
    r0, r1 = [jnp.sum(a, axis=0, keepdims=True) + b_ref[...] for a in acc]
    ridx = lax.broadcasted_iota(jnp.int32, (8, tn), 0)
    o_ref[...] = jnp.where(ridx == 0, r0, jnp.where(ridx == 1, r1, 0.0))


def _modulation(c_cols, w_mod, b_mod, layer):
    depth, d, n = w_mod.shape
    tn = 1536
    return pl.pallas_call(
        _mod_kernel,
        grid=(n // tn,),
        in_specs=[
            _const_spec((d, 128)),
            pl.BlockSpec((None, d, tn), lambda j: (layer, 0, j)),
            pl.BlockSpec((None, 1, tn), lambda j: (layer, 0, j)),
        ],
        out_specs=pl.BlockSpec((8, tn), lambda j: (0, j)),
        out_shape=jax.ShapeDtypeStruct((8, n), F32),
        compiler_params=_cparams(("arbitrary",)),
        name="modulation",
    )(c_cols, w_mod, b_mod.reshape(depth, 1, n))


NORM_RC = 128
IN_TN = 1536
UP_TN = 1024


def _norm_rows(x_ref, g_ref, sh_ref, sc_ref, hx_ref, row):
    tm = x_ref.shape[0]
    g = g_ref[...]
    sc1 = 1.0 + sc_ref[row:row + 1, :]
    sh = sh_ref[row:row + 1, :]

    def body(r, carry):
        r0 = pl.multiple_of(r * NORM_RC, NORM_RC)
        x = x_ref[pl.ds(r0, NORM_RC), :]
        ms = jnp.mean(x * x, axis=-1, keepdims=True)
        y = (x * lax.rsqrt(ms + EPS)) * g
        hx_ref[pl.ds(r0, NORM_RC), :] = (y * sc1 + sh).astype(BF16)
        return carry

    lax.fori_loop(0, tm // NORM_RC, body, 0)


def _norm_matmul_kernel(x_ref, g_ref, sh_ref, sc_ref, w_ref, o_ref, hx_ref, *, row, w_transposed):
    @pl.when(pl.program_id(1) == 0)
    def _():
        _norm_rows(x_ref, g_ref, sh_ref, sc_ref, hx_ref, row)

    contract = (((1,), (1 if w_transposed else 0,)), ((), ()))
    o_ref[...] = lax.dot_general(hx_ref[...], w_ref[...], contract, preferred_element_type=F32).astype(BF16)


def _norm_matmul(x2d, norm_g, mod, sh_blk, row, w_all, layer, tm, tn, name, w_transposed=False):
    n, d = x2d.shape
    if w_transposed:
        ncol = w_all.shape[1]
        w_spec = pl.BlockSpec((None, tn, d), lambda i, j: (layer, j, 0))
    else:
        ncol = w_all.shape[2]
        w_spec = pl.BlockSpec((None, d, tn), lambda i, j: (layer, 0, j))
    return pl.pallas_call(
        functools.partial(_norm_matmul_kernel, row=row, w_transposed=w_transposed),
        grid=(n // tm, ncol // tn),
        in_specs=[
            pl.BlockSpec((tm, d), lambda i, j: (i, 0)),
            _layer_spec((1, d), layer),
            pl.BlockSpec((8, d), lambda i, j: (0, sh_blk), pipeline_mode=pl.Buffered(1)),
            pl.BlockSpec((8, d), lambda i, j: (0, sh_blk + 1), pipeline_mode=pl.Buffered(1)),
            w_spec,
        ],
        out_specs=pl.BlockSpec((tm, tn), lambda i, j: (i, j)),
        out_shape=jax.ShapeDtypeStruct((n, ncol), BF16),
        scratch_shapes=[pltpu.VMEM((tm, d), BF16)],
        compiler_params=_cparams(("parallel", "arbitrary")),
        name=name,
    )(x2d, norm_g, mod, mod, w_all)


def _rope_rows(r, cos_ref, sin_ref):
    out = []
    for part in range(2):
        cs = cos_ref[part]
        sn = sin_ref[part]
        x1 = r[32 * part:32 * part + 16]
        x2 = r[32 * part + 16:32 * part + 32]
        out.append(x1 * cs - x2 * sn)
        out.append(x2 * cs + x1 * sn)
    return out


def _prep_kernel(p_ref, cos_ref, sin_ref, naqg_ref, nakg_ref, kvg_ref, mqg_ref, mkg_ref,
                 wkn_ref, wv_ref,
                 naqT_ref, nak_ref, navT_ref, mqT_ref, mk_ref, mvT_ref, qn_ref, kn_ref):
    tm = p_ref.shape[0]

    def head_norm_T(xt, g_ref, scale):
        x3 = xt.reshape(NA_HEADS, NA_HEAD_DIM, tm)
        ms = jnp.mean(x3 * x3, axis=1, keepdims=True)
        y = x3 * lax.rsqrt(ms + EPS) * g_ref[...][None]
        if scale != 1.0:
            y = y * scale
        return y.reshape(NA_W, tm)

    naq_t = p_ref[:, NAQ_WIN[0]:NAQ_WIN[1]].astype(F32).T[OFF:OFF + NA_W]
    naqT_ref[...] = head_norm_T(naq_t, naqg_ref, NA_HEAD_DIM ** -0.5 * LOG2E).astype(BF16)
    nak_t = p_ref[:, R_NA_K:R_NA_V].astype(F32).T
    nak_ref[...] = head_norm_T(nak_t, nakg_ref, 1.0).T.astype(BF16)
    navT_ref[...] = p_ref[:, R_NA_V:R_CKV].astype(F32).T.astype(BF16)

    mqg = mqg_ref[...][:MLA_QK]
    qscale = MLA_QK ** -0.5 * LOG2E
    mq_t = p_ref[:, MQ_WIN[0]:MQ_WIN[1]].astype(F32).T
    for h in range(MLA_HEADS):
        qt = mq_t[OFF + h * MLA_QK:OFF + (h + 1) * MLA_QK]
        ms = jnp.sum(qt * qt, axis=0, keepdims=True) * (1.0 / MLA_QK)
        y = qt * lax.rsqrt(ms + EPS) * mqg
        r0 = h * MLA_QPAD
        mqT_ref[r0:r0 + MLA_NOPE, :] = (y[:MLA_NOPE] * qscale).astype(BF16)
        parts = _rope_rows(y[MLA_NOPE:], cos_ref, sin_ref)
        for k, part in enumerate(parts):
            a = r0 + MLA_NOPE + 16 * k
            mqT_ref[a:a + 16, :] = (part * qscale).astype(BF16)
        mqT_ref[r0 + MLA_QK:r0 + MLA_QPAD, :] = jnp.zeros((MLA_QPAD - MLA_QK, tm), BF16)
        qnorm = jnp.sqrt(jnp.sum(y * y, axis=0, keepdims=True)) * qscale
        qn_ref[8 * h:8 * h + 8, :] = jnp.broadcast_to(qnorm, (8, tm))

    ct = p_ref[:, R_CKV:R_KROPE].astype(F32).T
    ms = jnp.mean(ct * ct, axis=0, keepdims=True)
    cn = (ct * lax.rsqrt(ms + EPS) * kvg_ref[...]).astype(BF16)
    mvT_ref[...] = jnp.dot(wv_ref[...], cn, preferred_element_type=F32).astype(BF16)
    knT = jnp.dot(wkn_ref[...], cn, preferred_element_type=F32)
    krT = p_ref[:, R_KROPE:R_KROPE + LANE].astype(F32).T[:MLA_ROPE]
    kr_ss = jnp.sum(krT * krT, axis=0, keepdims=True)
    mkg = mkg_ref[...]
    pad = (lax.broadcasted_iota(jnp.int32, (MLA_QPAD - MLA_QK, tm), 0) == 0).astype(F32)
    for h in range(MLA_HEADS):
        kn = knT[h * MLA_NOPE:(h + 1) * MLA_NOPE]
        ms = (jnp.sum(kn * kn, axis=0, keepdims=True) + kr_ss) * (1.0 / MLA_QK)
        rinv = lax.rsqrt(ms + EPS)
        kn = kn * rinv * mkg[:MLA_NOPE]
        kr = krT * rinv * mkg[MLA_NOPE:MLA_QK]
        knorm = jnp.sqrt(jnp.sum(kn * kn, axis=0, keepdims=True) + jnp.sum(kr * kr, axis=0, keepdims=True))
        kn_ref[8 * h:8 * h + 8, :] = jnp.broadcast_to(knorm, (8, tm))
        pieces = [kn] + _rope_rows(kr, cos_ref, sin_ref) + [pad]
        kT = jnp.concatenate(pieces, axis=0)
        mk_ref[:, h * MLA_QPAD:(h + 1) * MLA_QPAD] = kT.T.astype(BF16)


def _prep(pg, cosT, sinT, gains, wknT, wvT, layer, tm):
    n = pg.shape[0]
    naqg, nakg, kvg, mqg, mkg = gains
    return pl.pallas_call(
        _prep_kernel,
        grid=(n // tm,),
        in_specs=[
            pl.BlockSpec((tm, QKV_WIN), lambda i: (i, 0)),
            pl.BlockSpec((2, ROPE_NFREQ, tm), lambda i: (0, 0, i)),
            pl.BlockSpec((2, ROPE_NFREQ, tm), lambda i: (0, 0, i)),
            _layer_spec((NA_HEAD_DIM, 1), layer),
            _layer_spec((NA_HEAD_DIM, 1), layer),
            _layer_spec((MLA_KV_RANK, 1), layer),
            _layer_spec((MLA_QPAD, 1), layer),
            _layer_spec((MLA_QPAD, 1), layer),
            _layer_spec((MLA_HEADS * MLA_NOPE, MLA_KV_RANK), layer),
            _layer_spec((MLA_HEADS * MLA_V, MLA_KV_RANK), layer),
        ],
        out_specs=[
            pl.BlockSpec((NA_W, tm), lambda i: (0, i)),
            pl.BlockSpec((tm, NA_W), lambda i: (i, 0)),
            pl.BlockSpec((NA_W, tm), lambda i: (0, i)),
            pl.BlockSpec((MLA_HEADS * MLA_QPAD, tm), lambda i: (0, i)),
            pl.BlockSpec((tm, MLA_HEADS * MLA_QPAD), lambda i: (i, 0)),
            pl.BlockSpec((MLA_HEADS * MLA_V, tm), lambda i: (0, i)),
            pl.BlockSpec((MLA_HEADS * 8, tm), lambda i: (0, i)),
            pl.BlockSpec((MLA_HEADS * 8, tm), lambda i: (0, i)),
        ],
        out_shape=[
            jax.ShapeDtypeStruct((NA_W, n), BF16),
            jax.ShapeDtypeStruct((n, NA_W), BF16),
            jax.ShapeDtypeStruct((NA_W, n), BF16),
            jax.ShapeDtypeStruct((MLA_HEADS * MLA_QPAD, n), BF16),
            jax.ShapeDtypeStruct((n, MLA_HEADS * MLA_QPAD), BF16),
            jax.ShapeDtypeStruct((MLA_HEADS * MLA_V, n), BF16),
            jax.ShapeDtypeStruct((MLA_HEADS * 8, n), F32),
            jax.ShapeDtypeStruct((MLA_HEADS * 8, n), F32),
        ],
        compiler_params=_cparams(("parallel",)),
        name="qkv_prep",
    )(pg, cosT, sinT, naqg, nakg, kvg, mqg, mkg, wknT, wvT)


def _na_heads(q_ref, k_refs, vT_refs, bias_ref, o_ref, s_bufs):
    tq = q_ref.shape[1]
    upper = lax.broadcasted_iota(jnp.int32, (128, tq), 0) >= NA_HEAD_DIM

    def scores(h, s_ref):
        j, e = divmod(h, 2)
        q2 = q_ref[128 * j:128 * j + 128, :]
        qm = jnp.where(upper == bool(e), q2, jnp.zeros_like(q2))
        off = 0
        for b, k_ref in enumerate(k_refs):
            s = jnp.dot(k_ref[:, 128 * j:128 * j + 128], qm, preferred_element_type=F32)
            nk = k_ref.shape[0]
            if bias_ref is not None and b < len(k_refs) - 1:
                s = s + bias_ref[0, h, off:off + nk, :]
            s_ref[off:off + nk, :] = s
            off += nk

    def attend(h, s_ref):
        s = s_ref[...]
        m = s.max(axis=0, keepdims=True)
        p = jnp.exp2(s - m)
        l = p.sum(axis=0, keepdims=True)
        pb = p.astype(BF16)
        acc = jnp.zeros((NA_HEAD_DIM, tq), F32)
        off = 0
        for vT_ref in vT_refs:
            nk = vT_ref.shape[1]
            acc = acc + jnp.dot(vT_ref[NA_HEAD_DIM * h:NA_HEAD_DIM * (h + 1), :], pb[off:off + nk],
                                preferred_element_type=F32)
            off += nk
        return acc / l

    outs = []
    scores(0, s_bufs[0])
    for h in range(NA_HEADS):
        if h + 1 < NA_HEADS:
            scores(h + 1, s_bufs[(h + 1) % 2])
        outs.append(attend(h, s_bufs[h % 2]))
    o_ref[...] = jnp.concatenate(outs, axis=0).T.astype(BF16)


def _na_latent_kernel(q_ref, k0, k1, k2, kc, v0, v1, v2, vc, bias_ref, o_ref, s_a, s_b):
    _na_heads(q_ref, [k0, k1, k2, kc], [v0, v1, v2, vc], bias_ref, o_ref, (s_a, s_b))


def _na_ctx_kernel(q_ref, kc, vc, o_ref, s_a, s_b):
    _na_heads(q_ref, [kc], [vc], None, o_ref, (s_a, s_b))


def _na_latent(naqT, nak, navT, nakc, navcT, bias, layer):
    n = naqT.shape[1]
    nb = n // NA_TQ
    nctx = nakc.shape[0]

    def kb(o):
        return lambda i: (jnp.clip(i - 1, 0, nb - 3) + o, 0)

    def vb(o):
        return lambda i: (0, jnp.clip(i - 1, 0, nb - 3) + o)

    def variant(i):
        return (layer, jnp.where(i == 0, 0, jnp.where(i == nb - 1, 2, 1)), 0, 0, 0)

    return pl.pallas_call(
        _na_latent_kernel,
        grid=(nb,),
        in_specs=[
            pl.BlockSpec((NA_W, NA_TQ), lambda i: (0, i)),
            pl.BlockSpec((NA_TQ, NA_W), kb(0)),
            pl.BlockSpec((NA_TQ, NA_W), kb(1)),
            pl.BlockSpec((NA_TQ, NA_W), kb(2)),
            _const_spec((nctx, NA_W)),
            pl.BlockSpec((NA_W, NA_TQ), vb(0)),
            pl.BlockSpec((NA_W, NA_TQ), vb(1)),
            pl.BlockSpec((NA_W, NA_TQ), vb(2)),
            _const_spec((NA_W, nctx)),
            pl.BlockSpec((None, 1, NA_HEADS, NA_TK, NA_TQ), variant),
        ],
        out_specs=pl.BlockSpec((NA_TQ, NA_W), lambda i: (i, 0)),
        out_shape=jax.ShapeDtypeStruct((n, NA_W), BF16),
        scratch_shapes=[pltpu.VMEM((NA_TK + nctx, NA_TQ), F32), pltpu.VMEM((NA_TK + nctx, NA_TQ), F32)],
        compiler_params=_cparams(("parallel",)),
        name="na_latent",
    )(naqT, nak, nak, nak, nakc, navT, navT, navT, navcT, bias)


def _na_ctx(naqcT, nakc, navcT):
    nctx = nakc.shape[0]
    return pl.pallas_call(
        _na_ctx_kernel,
        grid=(1,),
        in_specs=[_const_spec((NA_W, nctx)), _const_spec((nctx, NA_W)), _const_spec((NA_W, nctx))],
        out_specs=pl.BlockSpec((nctx, NA_W), lambda i: (0, 0)),
        out_shape=jax.ShapeDtypeStruct((nctx, NA_W), BF16),
        scratch_shapes=[pltpu.VMEM((nctx, nctx), F32), pltpu.VMEM((nctx, nctx), F32)],
        compiler_params=_cparams(("arbitrary",)),
        name="na_ctx",
    )(naqcT, nakc, navcT)


def _na_bias_tables(rel_bias, rows):
    ndc = 2 * NA_WIN_W - 1
    col = np.arange(GRID_W)
    start_c = np.clip(col - NA_WIN_W // 2, 0, GRID_W - NA_WIN_W)
    in_col = (col[:, None] >= start_c[None, :]) & (col[:, None] < start_c[None, :] + NA_WIN_W)
    dc = col[:, None] - col[None, :] + (NA_WIN_W - 1)
    sel_c = (dc[:, :, None] == np.arange(ndc)).astype(np.float32)
    by_col = jnp.einsum('lhrc,kqc->lhrkq', rel_bias.astype(F32), sel_c, precision=lax.Precision.HIGHEST)
    by_col = jnp.where(in_col, by_col * LOG2E, NEG)
    masked = jnp.full(by_col.shape[:2] + (GRID_W, GRID_W), NEG, F32)
    tabs = []
    for r0, base in ((0, 0), (NA_QROWS, 0), (rows - NA_QROWS, rows - NA_KROWS)):
        key_rows = []
        for a in range(NA_KROWS):
            blocks = []
            for b in range(NA_QROWS):
                qr, kr = r0 + b, base + a
                start_r = min(max(qr - NA_WIN_H // 2, 0), rows - NA_WIN_H)
                inside = start_r <= kr < start_r + NA_WIN_H
                blocks.append(by_col[:, :, kr - qr + NA_WIN_H - 1] if inside else masked)
            key_rows.append(jnp.concatenate(blocks, axis=-1))
        tabs.append(jnp.concatenate(key_rows, axis=-2))
    return jnp.stack(tabs, axis=1)


def _mla_kernel(*refs, tk, has_ctx):
    if has_ctx:
        qT_ref, qn_ref, k_ref, vT_ref, kn_ref, kc_ref, vcT_ref, knc_ref, o_ref = refs[:9]
    else:
        qT_ref, qn_ref, k_ref, vT_ref, kn_ref, o_ref = refs[:6]
    m_ref, l_ref, acc_ref, al_ref, s_a, s_b, p_a, p_b, qs_ref = refs[-9:]
    tq = qT_ref.shape[1]
    nk = k_ref.shape[0] // tk

    kmax = jnp.max(kn_ref[0:1, :])
    if has_ctx:
        kmax = jnp.maximum(kmax, jnp.max(knc_ref[0:1, :]))
    bound = qn_ref[0:1, :] * (kmax * MLA_BOUND_MARGIN)
    shift_ok = jnp.max(bound) <= MLA_MAX_SHIFT

    @pl.when(shift_ok)
    def _():
        qs_ref[...] = qT_ref[...]
        row0 = lax.broadcasted_iota(jnp.int32, (16, tq), 0) == 0
        qs_ref[MLA_QK:MLA_QK + 16, :] = jnp.where(row0, -bound, 0.0).astype(BF16)

        def chunk(k_blk, vT_blk):
            p = jnp.exp2(jnp.dot(k_blk, qs_ref[...], preferred_element_type=F32))
            return p.sum(axis=0, keepdims=True), jnp.dot(vT_blk, p.astype(BF16), preferred_element_type=F32)

        if has_ctx:
            l0, a0 = chunk(kc_ref[...], vcT_ref[...])
            l_ref[...] = l0
            acc_ref[...] = a0
        else:
            l_ref[...] = jnp.zeros((1, tq), F32)
            acc_ref[...] = jnp.zeros((MLA_V, tq), F32)
        unroll = next(u for u in (8, 4, 2, 1) if nk % u == 0)

        def body(it, carry):
            l_sum = l_ref[...]
            a_sum = acc_ref[...]
            for u in range(unroll):
                k0 = pl.multiple_of((it * unroll + u) * tk, tk)
                lp, ap = chunk(k_ref[pl.ds(k0, tk), :], vT_ref[:, pl.ds(k0, tk)])
                l_sum = l_sum + lp
                a_sum = a_sum + ap
            l_ref[...] = l_sum
            acc_ref[...] = a_sum
            return carry

        lax.fori_loop(0, nk // unroll, body, 0)

    @pl.when(jnp.logical_not(shift_ok))
    def _():
        if has_ctx:
            s = jnp.dot(kc_ref[...], qT_ref[...], preferred_element_type=F32)
            m = s.max(axis=0, keepdims=True)
            p = jnp.exp2(s - m)
            m_ref[...] = m
            l_ref[...] = p.sum(axis=0, keepdims=True)
            acc_ref[...] = jnp.dot(vcT_ref[...], p.astype(BF16), preferred_element_type=F32)
        else:
            m_ref[...] = jnp.full((1, tq), NEG, F32)
            l_ref[...] = jnp.zeros((1, tq), F32)
            acc_ref[...] = jnp.zeros((MLA_V, tq), F32)

        def scores(chunk, s_ref):
            k0 = pl.multiple_of(chunk * tk, tk)
            s_ref[...] = jnp.dot(k_ref[pl.ds(k0, tk), :], qT_ref[...], preferred_element_type=F32)

        def softmax(s_ref, p_ref):
            s = s_ref[...]
            m_old = m_ref[...]
            m_new = jnp.maximum(m_old, s.max(axis=0, keepdims=True))
            alpha = jnp.exp2(m_old - m_new)
            p = jnp.exp2(s - m_new)
            l_ref[...] = alpha * l_ref[...] + p.sum(axis=0, keepdims=True)
            m_ref[...] = m_new
            p_ref[...] = p.astype(BF16)
            return alpha

        def values(chunk, p_ref, alpha):
            k0 = pl.multiple_of(chunk * tk, tk)
            acc_ref[...] = alpha * acc_ref[...] + jnp.dot(vT_ref[:, pl.ds(k0, tk)], p_ref[...],
                                                          preferred_element_type=F32)

        if nk % 2 == 0:
            scores(0, s_a)
            p_b[...] = jnp.zeros(p_b.shape, BF16)
            al_ref[...] = jnp.ones((1, tq), F32)

            def body(it, carry):
                c0 = 2 * it
                scores(c0 + 1, s_b)
                a0 = softmax(s_a, p_a)
                values(jnp.maximum(c0 - 1, 0), p_b, al_ref[...])
                scores(jnp.minimum(c0 + 2, nk - 1), s_a)
                a1 = softmax(s_b, p_b)
                values(c0, p_a, a0)
                al_ref[...] = a1
                return carry

            lax.fori_loop(0, nk // 2, body, 0)
            values(nk - 1, p_b, al_ref[...])
        else:
            for c in range(nk):
                scores(c, s_a)
                values(c, p_a, softmax(s_a, p_a))

    o_ref[...] = (acc_ref[...] / l_ref[...]).T.astype(BF16)


def _mla_attention(mqT, qn, mk, mvT, kn, ctx_kv, tq, tk):
    nq = mqT.shape[1]
    nkeys = mk.shape[0]
    has_ctx = ctx_kv is not None
    in_specs = [
        pl.BlockSpec((MLA_QPAD, tq), lambda h, i: (h, i)),
        pl.BlockSpec((8, tq), lambda h, i: (h, i)),
        pl.BlockSpec((nkeys, MLA_QPAD), lambda h, i: (0, h)),
        pl.BlockSpec((MLA_V, nkeys), lambda h, i: (h, 0)),
        pl.BlockSpec((8, nkeys), lambda h, i: (h, 0)),
    ]
    args = [mqT, qn, mk, mvT, kn]
    if has_ctx:
        kc, vcT, knc = ctx_kv
        nctx = kc.shape[0]
        in_specs += [
            pl.BlockSpec((nctx, MLA_QPAD), lambda h, i: (0, h)),
            pl.BlockSpec((MLA_V, nctx), lambda h, i: (h, 0)),
            pl.BlockSpec((8, nctx), lambda h, i: (h, 0)),
        ]
        args += [kc, vcT, knc]
    return pl.pallas_call(
        functools.partial(_mla_kernel, tk=tk, has_ctx=has_ctx),
        grid=(MLA_HEADS, nq // tq),
        in_specs=in_specs,
        out_specs=pl.BlockSpec((tq, MLA_V), lambda h, i: (i, h)),
        out_shape=jax.ShapeDtypeStruct((nq, MLA_HEADS * MLA_V), BF16),
        scratch_shapes=[pltpu.VMEM((1, tq), F32), pltpu.VMEM((1, tq), F32), pltpu.VMEM((MLA_V, tq), F32),
                        pltpu.VMEM((1, tq), F32),
                        pltpu.VMEM((tk, tq), F32), pltpu.VMEM((tk, tq), F32),
                        pltpu.VMEM((tk, tq), BF16), pltpu.VMEM((tk, tq), BF16),
                        pltpu.VMEM((MLA_QPAD, tq), BF16)],
        compiler_params=_cparams(("parallel", "arbitrary")),
        name="mla_attention",
    )(*args)


def _conv3_rows(x, prev_row, next_row, w):
    tm = x.shape[0]
    rows = lax.broadcasted_iota(jnp.int32, (tm, 1), 0)
    xm1 = jnp.where(rows == 0, prev_row, pltpu.roll(x, 1, 0))
    xp1 = jnp.where(rows == tm - 1, next_row, pltpu.roll(x, tm - 1, 0))
    return w[0:1, :] * xm1 + w[1:2, :] * x + w[2:3, :] * xp1


HALO = 16


def _merge_kernel(ya_ref, yb_ref, ubc_ref, prev_ref, next_ref, g0_ref, g1g_ref, g2_ref, g3_ref, x_ref, g1_ref,
                  cw_ref, wa_ref, wb_ref, wc_ref, wo_ref, o_ref, *, row):
    i = pl.program_id(0)
    nblk = pl.num_programs(0)

    def conv_windows(ref, rows):
        blk = ref[...].astype(F32)[rows]
        return [blk[:, g * CONV_CH:g * CONV_CH + CONV_W] for g in range(3)]

    u, gb, gc = conv_windows(ubc_ref, slice(None))
    pu, _, pc = conv_windows(prev_ref, slice(HALO - 1, HALO))
    nu, _, nc = conv_windows(next_ref, slice(0, 1))
    cu_prev = pc * pu * (i > 0).astype(F32)
    cu_next = nc * nu * (i < nblk - 1).astype(F32)
    yc = gb * _conv3_rows(gc * u, cu_prev, cu_next, cw_ref[...])
    lane = lax.broadcasted_iota(jnp.int32, yc.shape, 1)
    yc = jnp.where((lane >= OFF) & (lane < OFF + CONV_CH), yc, 0.0).astype(BF16)

    a = jnp.dot(ya_ref[...], wa_ref[...], preferred_element_type=F32)
    b = jnp.dot(yb_ref[...], wb_ref[...], preferred_element_type=F32)
    c = jnp.dot(yc, wc_ref[...], preferred_element_type=F32)

    def sig(ref, cols=slice(None)):
        return jax.nn.sigmoid(ref[:, cols].astype(F32))

    merged = sig(g0_ref) * a + sig(g1g_ref) * b + sig(g2_ref) * c
    first = slice(0, LANE)
    wrapped = sig(g1g_ref, first) * a[:, first] + sig(g2_ref, first) * b[:, first] + sig(g3_ref) * c[:, first]
    lane = lax.broadcasted_iota(jnp.int32, wrapped.shape, 1)
    head = jnp.where(lane < OFF, wrapped, merged[:, first])
    merged = jnp.concatenate([head, merged[:, LANE:]], axis=1)
    mix = jnp.dot(merged.astype(BF16), wo_ref[...], preferred_element_type=F32)
    o_ref[...] = x_ref[...] + g1_ref[row:row + 1, :] * mix


def _merge(ya, yb, pg, x2d, mod, g1_blk, row, conv_w8, wa, wb, wc, wo, layer, tm):
    n, d = x2d.shape
    ubc_blk = CONV_WIN0 // CONV_WIN
    gblk = GATE_WIN0 // d
    hb = tm // HALO
    last = n // HALO - 1
    return pl.pallas_call(
        functools.partial(_merge_kernel, row=row),
        grid=(n // tm,),
        in_specs=[
            pl.BlockSpec((tm, NA_W), lambda i: (i, 0)),
            pl.BlockSpec((tm, MLA_HEADS * MLA_V), lambda i: (i, 0)),
            pl.BlockSpec((tm, CONV_WIN), lambda i: (i, ubc_blk)),
            pl.BlockSpec((HALO, CONV_WIN), lambda i: (jnp.maximum(i * hb - 1, 0), ubc_blk)),
            pl.BlockSpec((HALO, CONV_WIN), lambda i: (jnp.minimum((i + 1) * hb, last), ubc_blk)),
            pl.BlockSpec((tm, d), lambda i: (i, gblk)),
            pl.BlockSpec((tm, d), lambda i: (i, gblk + 1)),
            pl.BlockSpec((tm, d), lambda i: (i, gblk + 2)),
            pl.BlockSpec((tm, LANE), lambda i: (i, (GATE_WIN0 + G_COLS) // LANE)),
            pl.BlockSpec((tm, d), lambda i: (i, 0)),
            pl.BlockSpec((8, d), lambda i: (0, g1_blk), pipeline_mode=pl.Buffered(1)),
            _layer_spec((8, CONV_W), layer),
            _layer_spec(wa.shape[1:], layer),
            _layer_spec(wb.shape[1:], layer),
            _layer_spec(wc.shape[1:], layer),
            _layer_spec(wo.shape[1:], layer),
        ],
        out_specs=pl.BlockSpec((tm, d), lambda i: (i, 0)),
        out_shape=jax.ShapeDtypeStruct((n, d), F32),
        compiler_params=_cparams(("parallel",)),
        name="merge_out",
    )(ya, yb, pg, pg, pg, pg, pg, pg, pg, x2d, mod, conv_w8, wa, wb, wc, wo)


FFN_KC = 256


def _ffn_down_kernel(a_ref, prev_ref, next_ref, val_ref, cw_ref, wd_ref, x_ref, g2_ref, o_ref,
                     acc_ref, h_a, h_b, *, row):
    i = pl.program_id(0)
    nblk = pl.num_programs(0)
    has_prev = (i > 0).astype(F32)
    has_next = (i < nblk - 1).astype(F32)
    nchunks = D_FF // FFN_KC

    def hidden(c, h_ref):
        sl = slice(c * FFN_KC, (c + 1) * FFN_KC)
        w = cw_ref[:, sl]
        a = a_ref[:, sl].astype(F32)
        pr = prev_ref[:, sl].astype(F32)[HALO - 1:HALO, :] * has_prev
        nx = next_ref[:, sl].astype(F32)[0:1, :] * has_next
        a = _conv3_rows(a, pr, nx, w) + w[3:4, :]
        h_ref[...] = (a * jax.nn.sigmoid(a) * val_ref[:, sl].astype(F32)).astype(BF16)

    bufs = (h_a, h_b)
    hidden(0, h_a)
    for c in range(nchunks):
        if c + 1 < nchunks:
            hidden(c + 1, bufs[(c + 1) % 2])
        part = jnp.dot(bufs[c % 2][...], wd_ref[c * FFN_KC:(c + 1) * FFN_KC, :], preferred_element_type=F32)
        if c == 0:
            acc_ref[...] = part
        else:
            acc_ref[...] += part
    o_ref[...] = x_ref[...] + g2_ref[row:row + 1, :] * acc_ref[...]


def _ffn_down(av, x2d, mod, g2_blk, row, cw8, wd, layer, tm):
    n, d = x2d.shape
    hb = tm // HALO
    last = n // HALO - 1
    return pl.pallas_call(
        functools.partial(_ffn_down_kernel, row=row),
        grid=(n // tm,),
        in_specs=[
            pl.BlockSpec((tm, D_FF), lambda i: (i, 0)),
            pl.BlockSpec((HALO, D_FF), lambda i: (jnp.maximum(i * hb - 1, 0), 0)),
            pl.BlockSpec((HALO, D_FF), lambda i: (jnp.minimum((i + 1) * hb, last), 0)),
            pl.BlockSpec((tm, D_FF), lambda i: (i, 1)),
            _layer_spec((8, D_FF), layer),
            _layer_spec((D_FF, d), layer),
            pl.BlockSpec((tm, d), lambda i: (i, 0)),
            pl.BlockSpec((8, d), lambda i: (0, g2_blk), pipeline_mode=pl.Buffered(1)),
        ],
        out_specs=pl.BlockSpec((tm, d), lambda i: (i, 0)),
        out_shape=jax.ShapeDtypeStruct((n, d), F32),
        scratch_shapes=[pltpu.VMEM((tm, d), F32), pltpu.VMEM((tm, FFN_KC), BF16), pltpu.VMEM((tm, FFN_KC), BF16)],
        compiler_params=_cparams(("parallel",)),
        name="ffn_down",
    )(av, av, av, av, cw8, wd, x2d, mod)


W_IN_TR = 448


def _cast_pad_kernel(w_ref, o_ref, *, n_real):
    w = w_ref[...].astype(BF16)
    o_ref[...] = jnp.where(pl.program_id(1) < n_real, w, jnp.zeros_like(w))


def _cast_pad_w_in(w_in):
    depth, d, ncol = w_in.shape
    assert ncol % W_IN_TR == 0 and PG_COLS % W_IN_TR == 0
    n_real = ncol // W_IN_TR
    return pl.pallas_call(
        functools.partial(_cast_pad_kernel, n_real=n_real),
        grid=(depth, PG_COLS // W_IN_TR),
        in_specs=[pl.BlockSpec((None, W_IN_TR, d), lambda l, i: (l, jnp.minimum(i, n_real - 1), 0))],
        out_specs=pl.BlockSpec((None, W_IN_TR, d), lambda l, i: (l, i, 0)),
        out_shape=jax.ShapeDtypeStruct((depth, PG_COLS, d), BF16),
        compiler_params=_cparams(("parallel", "parallel")),
        name="cast_w_in",
    )(jnp.swapaxes(w_in, 1, 2))


def _pad_rows(a, rows):
    return jnp.concatenate([a, jnp.zeros((a.shape[0], rows - a.shape[1], a.shape[2]), a.dtype)], axis=1)


def _rope_tables_T(n_tok):
    t = jnp.arange(n_tok)
    row = (t // GRID_W).astype(F32)
    col = (t % GRID_W).astype(F32)
    inv = ROPE_BASE ** (-jnp.arange(ROPE_NFREQ, dtype=F32) / ROPE_NFREQ)
    ang = jnp.stack([inv[:, None] * row[None, :], inv[:, None] * col[None, :]], axis=0)
    return jnp.cos(ang), jnp.sin(ang)


def kernel(x, c, ctx, c_ctx, w_mod, b_mod, norm1_g, w_in, na_q_g, na_k_g, na_rel_bias, mla_kv_g, w_kv_up,
           mla_q_g, mla_k_g, conv_w, w_br_a, w_br_b, w_br_c, w_o, norm2_g, w_up, ffn_conv_w, ffn_conv_b, w_down):
    bsz, seq, d = x.shape
    nctx = ctx.shape[1]
    depth = w_mod.shape[0]
    rows = seq // GRID_W
    assert bsz == 1 and d == D_MODEL and seq % NA_TQ == 0 and rows >= NA_KROWS and nctx % 256 == 0

    xs = x[0]
    cs = ctx[0]
    tm_x = min(1024, seq)
    tm_half = min(512, seq)
    tm_merge = min(256, seq)
    tq_mla = min(1024, seq)
    tk_mla = min(1024, seq)

    cos_x, sin_x = _rope_tables_T(seq)
    cos_c = jnp.ones((2, ROPE_NFREQ, nctx), F32)
    sin_c = jnp.zeros((2, ROPE_NFREQ, nctx), F32)
    c_cols = jnp.zeros((d, 128), F32).at[:, 0].set(c[0]).at[:, 1].set(c_ctx)

    w_in_all = _cast_pad_w_in(w_in)
    kv_up = w_kv_up.reshape(depth, MLA_KV_RANK, MLA_HEADS, MLA_NOPE + MLA_V)
    wknT = jnp.swapaxes(kv_up[..., :MLA_NOPE].reshape(depth, MLA_KV_RANK, -1), 1, 2).astype(BF16)
    wvT = jnp.swapaxes(kv_up[..., MLA_NOPE:].reshape(depth, MLA_KV_RANK, -1), 1, 2).astype(BF16)
    gains = (na_q_g[..., None], na_k_g[..., None], mla_kv_g[..., None],
             _pad_rows(mla_q_g[..., None], MLA_QPAD), _pad_rows(mla_k_g[..., None], MLA_QPAD))
    bias = _na_bias_tables(na_rel_bias, rows)
    conv_w8 = jnp.pad(_pad_rows(conv_w, 8), ((0, 0), (0, 0), (OFF, OFF)))
    ffn_cw8 = _pad_rows(jnp.concatenate([ffn_conv_w, ffn_conv_b[:, None, :]], axis=1), 8)
    wa, wb, wc = (jnp.roll(w, OFF, axis=2).astype(BF16) for w in (w_br_a, w_br_b, w_br_c))
    wc = jnp.pad(wc, ((0, 0), (OFF, OFF), (0, 0)))
    wo = jnp.roll(w_o, OFF, axis=1).astype(BF16)
    w_up_all, w_down_all = w_up.astype(BF16), w_down.astype(BF16)
    n1g = norm1_g[:, None, :]
    n2g = norm2_g[:, None, :]

    for l in range(depth):
        last = l == depth - 1
        mod = _modulation(c_cols, w_mod, b_mod, l)

        pgc = _norm_matmul(cs, n1g, mod, 0, 1, w_in_all, l, nctx, IN_TN, "in_proj", w_transposed=True)
        naqcT, nakc, navcT, mqcT, mkc, mvcT, qnc, knc = _prep(pgc, cos_c, sin_c, gains, wknT, wvT, l, nctx)
        pgx = _norm_matmul(xs, n1g, mod, 0, 0, w_in_all, l, tm_x, IN_TN, "in_proj", w_transposed=True)
        naqT, nak, navT, mqT, mk, mvT, qn, kn = _prep(pgx, cos_x, sin_x, gains, wknT, wvT, l, tm_half)

        ya = _na_latent(naqT, nak, navT, nakc, navcT, bias, l)
        yb = _mla_attention(mqT, qn, mk, mvT, kn, (mkc, mvcT, knc), tq_mla, tk_mla)
        xs = _merge(ya, yb, pgx, xs, mod, 2, 0, conv_w8, wa, wb, wc, wo, l, tm_merge)
        av = _norm_matmul(xs, n2g, mod, 3, 0, w_up_all, l, tm_x, UP_TN, "ffn_up")
        xs = _ffn_down(av, xs, mod, 5, 0, ffn_cw8, w_down_all, l, tm_merge)

        if not last:
            yac = _na_ctx(naqcT, nakc, navcT)
            ybc = _mla_attention(mqcT, qnc, mkc, mvcT, knc, None, nctx, nctx)
            cs = _merge(yac, ybc, pgc, cs, mod, 2, 1, conv_w8, wa, wb, wc, wo, l, nctx)
            avc = _norm_matmul(cs, n2g, mod, 3, 1, w_up_all, l, nctx, UP_TN, "ffn_up")
            cs = _ffn_down(avc, cs, mod, 5, 1, ffn_cw8, w_down_all, l, nctx)

    return xs[None]
```
